```python
import jax, jax.numpy as jnp
from jax import lax
import numpy as np

D_MODEL = 1024
BATCH = 4
SEQ = 4096
DEPTH = 2

PLE_DIM = 256
N_HEADS = 8
HEAD_DIM = 64
ATTN_WIDTH = N_HEADS * HEAD_DIM
CONV_WIDTH = D_MODEL // 2
CONV_K = 3
D_FF = 2816
BLOCK = 256
TOP_K = 3
Q_CHUNK = 64
ROPE_THETA = 10000.0
EPS = 1e-6
IN_COLS = 3 * CONV_WIDTH + 3 * ATTN_WIDTH + 2 * D_MODEL

kernel_name = "hybrid_conv_moba_macaron_ple"


def rmsnorm(x, g):
    xf = x.astype(jnp.float32)
    y = xf * lax.rsqrt(jnp.mean(xf * xf, axis=-1, keepdims=True) + EPS)
    return (y * g.astype(jnp.float32)).astype(x.dtype)


def swiglu(u, w_gate, w_up, w_down):
    return (jax.nn.silu(u @ w_gate) * (u @ w_up)) @ w_down


def rope(x, positions):
    half = HEAD_DIM // 2
    inv_freq = ROPE_THETA ** (-jnp.arange(half, dtype=jnp.float32) / half)
    ang = positions.astype(jnp.float32)[:, :, None] * inv_freq
    cos = jnp.cos(ang)[:, :, None, :]
    sin = jnp.sin(ang)[:, :, None, :]
    xf = x.astype(jnp.float32)
    x1, x2 = xf[..., :half], xf[..., half:]
    out = jnp.concatenate([x1 * cos - x2 * sin, x2 * cos + x1 * sin], axis=-1)
    return out.astype(x.dtype)


def short_conv_mixer(b_gate, c_gate, xc, conv_w):
    u = c_gate * xc
    s = u.shape[1]
    up = jnp.pad(u, ((0, 0), (CONV_K - 1, 0), (0, 0)))
    conv = up[:, 0:s] * conv_w[0]
    for j in range(1, CONV_K):
        conv = conv + up[:, j:j + s] * conv_w[j]
    return b_gate * conv


def moba_attention(q, k, v):
    b, h, s, dh = q.shape
    nb = max(-(-s // BLOCK), TOP_K)
    pad = nb * BLOCK - s
    kp = jnp.pad(k, ((0, 0), (0, 0), (0, pad), (0, 0)))
    vp = jnp.pad(v, ((0, 0), (0, 0), (0, pad), (0, 0)))
    kb = kp.reshape(b, h, nb, BLOCK, dh)
    vb = vp.reshape(b, h, nb, BLOCK, dh)
    kmean = jnp.mean(kb.astype(jnp.float32), axis=3).astype(k.dtype)
    scale = HEAD_DIM ** -0.5
    bi = jnp.arange(b)[:, None, None, None]
    hi = jnp.arange(h)[None, :, None, None]
    n_chunks = s // Q_CHUNK

    def chunk(c):
        start = c * Q_CHUNK
        blk = start // BLOCK
        qc = lax.dynamic_slice_in_dim(q, start, Q_CHUNK, axis=2)
        gate = jnp.einsum('bhqd,bhnd->bhqn', qc, kmean).astype(jnp.float32)
        gate = jnp.where(jnp.arange(nb) < blk, gate, -jnp.inf)
        _, idx = lax.top_k(gate, TOP_K)
        k_sel = kb[bi, hi, idx]
        v_sel = vb[bi, hi, idx]
        s_sel = jnp.einsum('bhqd,bhqjkd->bhqjk', qc, k_sel).astype(jnp.float32) * scale
        slot_ok = jnp.arange(TOP_K) < blk
        s_sel = jnp.where(slot_ok[:, None], s_sel, -jnp.inf).reshape(b, h, Q_CHUNK, TOP_K * BLOCK)
        k_own = lax.dynamic_slice_in_dim(kp, blk * BLOCK, BLOCK, axis=2)
        v_own = lax.dynamic_slice_in_dim(vp, blk * BLOCK, BLOCK, axis=2)
        s_own = jnp.einsum('bhqd,bhkd->bhqk', qc, k_own).astype(jnp.float32) * scale
        qpos = start + jnp.arange(Q_CHUNK)
        kpos = blk * BLOCK + jnp.arange(BLOCK)
        s_own = jnp.where(kpos[None, :] <= qpos[:, None], s_own, -jnp.inf)
        probs = jax.nn.softmax(jnp.concatenate([s_sel, s_own], axis=-1), axis=-1).astype(v.dtype)
        p_sel = probs[..., :TOP_K * BLOCK].reshape(b, h, Q_CHUNK, TOP_K, BLOCK)
        p_own = probs[..., TOP_K * BLOCK:]
        return (jnp.einsum('bhqjk,bhqjkd->bhqd', p_sel, v_sel)
                + jnp.einsum('bhqk,bhkd->bhqd', p_own, v_own))

    outs = lax.map(chunk, jnp.arange(n_chunks))
    return outs.transpose(1, 2, 0, 3, 4).reshape(b, h, s, dh)


def token_mixing(u, positions, w_in, conv_w, w_conv_out, w_attn_out, w_o):
    b, s, _ = u.shape
    proj = u @ w_in
    cuts = [CONV_WIDTH, 2 * CONV_WIDTH, 3 * CONV_WIDTH,
            3 * CONV_WIDTH + ATTN_WIDTH, 3 * CONV_WIDTH + 2 * ATTN_WIDTH,
            3 * CONV_WIDTH + 3 * ATTN_WIDTH, 3 * CONV_WIDTH + 3 * ATTN_WIDTH + D_MODEL]
    b_gate, c_gate, xc, q, k, v, g_conv, g_attn = jnp.split(proj, cuts, axis=-1)
    y_conv = short_conv_mixer(b_gate, c_gate, xc, conv_w) @ w_conv_out
    q = rope(q.reshape(b, s, N_HEADS, HEAD_DIM), positions).transpose(0, 2, 1, 3)
    k = rope(k.reshape(b, s, N_HEADS, HEAD_DIM), positions).transpose(0, 2, 1, 3)
    v = v.reshape(b, s, N_HEADS, HEAD_DIM).transpose(0, 2, 1, 3)
    o = moba_attention(q, k, v).transpose(0, 2, 1, 3).reshape(b, s, ATTN_WIDTH)
    y_attn = o @ w_attn_out
    merged = jax.nn.sigmoid(g_conv) * y_conv + jax.nn.sigmoid(g_attn) * y_attn
    return merged @ w_o


def _w(key, shape, fan_in):
    return jax.random.normal(key, shape, jnp.float32) * (fan_in ** -0.5)


def _gain(key, shape):
    return 1.0 + 0.02 * jax.random.normal(key, shape, jnp.float32)


def setup_inputs(seed: int = 0) -> dict:
    key = jax.random.key(seed)
    ks = jax.random.split(key, 24)
    offsets = jax.random.randint(ks[2], (BATCH, 1), 0, 1024, dtype=jnp.int32)
    positions = (offsets + jnp.arange(SEQ, dtype=jnp.int32)[None, :]).astype(jnp.int32)
    return {
        "x": jax.random.normal(ks[0], (BATCH, SEQ, D_MODEL), jnp.float32),
        "p": jax.random.normal(ks[1], (DEPTH, BATCH, SEQ, PLE_DIM), jnp.float32),
        "positions": positions,
        "w_in": _w(ks[3], (DEPTH, D_MODEL, IN_COLS), D_MODEL),
        "conv_w": _w(ks[4], (DEPTH, CONV_K, CONV_WIDTH), CONV_K),
        "w_conv_out": _w(ks[5], (DEPTH, CONV_WIDTH, D_MODEL), CONV_WIDTH),
        "w_attn_out": _w(ks[6], (DEPTH, ATTN_WIDTH, D_MODEL), ATTN_WIDTH),
        "w_o": _w(ks[7], (DEPTH, D_MODEL, D_MODEL), D_MODEL),
        "ffn1_gate": _w(ks[8], (DEPTH, D_MODEL, D_FF), D_MODEL),
        "ffn1_up": _w(ks[9], (DEPTH, D_MODEL, D_FF), D_MODEL),
        "ffn1_down": _w(ks[10], (DEPTH, D_FF, D_MODEL), D_FF),
        "ffn2_gate": _w(ks[11], (DEPTH, D_MODEL, D_FF), D_MODEL),
        "ffn2_up": _w(ks[12], (DEPTH, D_MODEL, D_FF), D_MODEL),
        "ffn2_down": _w(ks[13], (DEPTH, D_FF, D_MODEL), D_FF),
        "norm_ffn1": _gain(ks[14], (DEPTH, D_MODEL)),
        "norm_mix": _gain(ks[15], (DEPTH, D_MODEL)),
        "norm_ffn2": _gain(ks[16], (DEPTH, D_MODEL)),
        "norm_ple": _gain(ks[17], (DEPTH, D_MODEL)),
        "w_ple_gate": _w(ks[18], (DEPTH, D_MODEL, D_MODEL), D_MODEL),
        "w_ple_proj": _w(ks[19], (DEPTH, PLE_DIM, D_MODEL), PLE_DIM),
        "norm_final": _gain(ks[20], (D_MODEL,)),
    }


def reference(x, p, positions, w_in, conv_w, w_conv_out, w_attn_out, w_o,
              ffn1_gate, ffn1_up, ffn1_down, ffn2_gate, ffn2_up, ffn2_down,
              norm_ffn1, norm_mix, norm_ffn2, norm_ple, w_ple_gate, w_ple_proj,
              norm_final):
    h = x
    for i in range(DEPTH):
        h = h + 0.5 * swiglu(rmsnorm(h, norm_ffn1[i]), ffn1_gate[i], ffn1_up[i], ffn1_down[i])
        h = h + token_mixing(rmsnorm(h, norm_mix[i]), positions, w_in[i], conv_w[i],
                             w_conv_out[i], w_attn_out[i], w_o[i])
        h = h + 0.5 * swiglu(rmsnorm(h, norm_ffn2[i]), ffn2_gate[i], ffn2_up[i], ffn2_down[i])
        h = h + jax.nn.sigmoid(rmsnorm(h, norm_ple[i]) @ w_ple_gate[i]) * (p[i] @ w_ple_proj[i])
    return rmsnorm(h, norm_final)
```

```python
import functools

import jax
import jax.numpy as jnp
from jax import lax
from jax.experimental import pallas as pl
from jax.experimental.pallas import tpu as pltpu

N_HEADS = 8
HEAD_DIM = 64
HALF_DIM = HEAD_DIM // 2
MOBA_BLOCK = 256
TOP_K = 3
ROPE_THETA = 10000.0
EPS = 1e-6

LANES = 128
SUBLANES = 8
HEADS_PER_GROUP = LANES // HEAD_DIM
VMEM_LIMIT_BYTES = 56 * 1024 * 1024
MASKED = -1e30

TOKEN_TILE = 512
FF_CHUNK = 256

F32 = jnp.float32
BF16 = jnp.bfloat16


def _dot(a, b):
    return jnp.dot(a, b, preferred_element_type=F32)


def _dot_nt(a, b):
    return lax.dot_general(a, b, (((1,), (1,)), ((), ())), preferred_element_type=F32)


def _rmsnorm(x, gain):
    ms = jnp.mean(x * x, axis=-1, keepdims=True)
    return x * lax.rsqrt(ms + EPS) * gain


def _sigmoid(x):
    return 1.0 / (1.0 + jnp.exp(-x))


def _resident(shape):
    nd = len(shape)
    return pl.BlockSpec(shape, lambda *_: (0,) * nd, pipeline_mode=pl.Buffered(1))


def _params(*semantics):
    return pltpu.CompilerParams(dimension_semantics=semantics, vmem_limit_bytes=VMEM_LIMIT_BYTES)


def _rope_table_kernel(pos_ref, inv_freq_ref, cos_ref, sin_ref):
    ang = pos_ref[0].astype(F32) * inv_freq_ref[...]
    lane = lax.broadcasted_iota(jnp.int32, (1, LANES), 1)
    first_half = (lane % HEAD_DIM) < HALF_DIM
    cos_ref[0] = jnp.cos(ang)
    sin = jnp.sin(ang)
    sin_ref[0] = jnp.where(first_half, -sin, sin)


def _rope_tables(positions, inv_freq):
    b, s = positions.shape
    tile = min(TOKEN_TILE, s)
    spec = pl.BlockSpec((1, tile, LANES), lambda i, j: (i, j, 0))
    return pl.pallas_call(
        _rope_table_kernel,
        grid=(b, s // tile),
        in_specs=[pl.BlockSpec((1, tile, 1), lambda i, j: (i, j, 0)), _resident((1, LANES))],
        out_specs=[spec, spec],
        out_shape=[jax.ShapeDtypeStruct((b, s, LANES), F32)] * 2,
        compiler_params=_params("parallel", "parallel"),
        name="rope_tables",
    )(positions.reshape(b, s, 1), inv_freq)


def _ffn_kernel(h_ref, gain_ref, wg_ref, wu_ref, wd_ref, o_ref, act_ref):
    x = h_ref[...]
    n = _rmsnorm(x, gain_ref[...]).astype(BF16)
    d_ff = wg_ref.shape[1]
    for c in range(d_ff // FF_CHUNK):
        cols = slice(c * FF_CHUNK, (c + 1) * FF_CHUNK)
        g = _dot(n, wg_ref[:, cols])
        u = _dot(n, wu_ref[:, cols])
        act_ref[:, cols] = (g * _sigmoid(g) * u).astype(BF16)
    o_ref[...] = x + 0.5 * _dot(act_ref[...], wd_ref[...])


def _ffn(h, gain, wg, wu, wd):
    t, d = h.shape
    d_ff = wg.shape[1]
    tile = min(TOKEN_TILE, t)
    row = pl.BlockSpec((tile, d), lambda i: (i, 0))
    return pl.pallas_call(
        _ffn_kernel,
        grid=(t // tile,),
        in_specs=[row, _resident((1, d)), _resident((d, d_ff)), _resident((d, d_ff)), _resident((d_ff, d))],
        out_specs=row,
        out_shape=jax.ShapeDtypeStruct((t, d), F32),
        scratch_shapes=[pltpu.VMEM((tile, d_ff), BF16)],
        compiler_params=_params("parallel"),
        name="ffn",
    )(h, gain, wg, wu, wd)


def _mix_in_kernel(h_ref, gain_ref, w_in_ref, conv_w_ref, w_conv_out_ref, cos_ref, sin_ref,
                   q_ref, k_ref, v_ref, mc_ref, ga_ref, u_ref, *, conv_width, attn_width):
    tile = h_ref.shape[1]
    cw, aw = conv_width, attn_width
    d = h_ref.shape[2]
    n = _rmsnorm(h_ref[0], gain_ref[...]).astype(BF16)

    bcx = _dot(n, w_in_ref[:, 0:3 * cw])
    b_gate, c_gate, xc = bcx[:, 0:cw], bcx[:, cw:2 * cw], bcx[:, 2 * cw:3 * cw]
    u = c_gate * xc

    @pl.when(pl.program_id(1) == 0)
    def _():
        u_ref[0:SUBLANES, :] = jnp.zeros((SUBLANES, cw), F32)

    u_ref[SUBLANES:SUBLANES + tile, :] = u
    u_prev1 = u_ref[SUBLANES - 1:SUBLANES - 1 + tile, :]
    u_prev2 = u_ref[SUBLANES - 2:SUBLANES - 2 + tile, :]
    conv_w = conv_w_ref[...]
    conv = u_prev2 * conv_w[0:1, :] + u_prev1 * conv_w[1:2, :] + u * conv_w[2:3, :]
    u_ref[0:SUBLANES, :] = u[tile - SUBLANES:tile, :]
    y_conv = _dot((b_gate * conv).astype(BF16), w_conv_out_ref[...])

    base = 3 * cw + 3 * aw
    g_conv = _dot(n, w_in_ref[:, base:base + d])
    mc_ref[0] = (_sigmoid(g_conv) * y_conv).astype(mc_ref.dtype)
    g_attn = _dot(n, w_in_ref[:, base + d:base + 2 * d])
    ga_ref[0] = _sigmoid(g_attn).astype(ga_ref.dtype)

    qkv = _dot(n, w_in_ref[:, 3 * cw:3 * cw + 3 * aw])
    cos = cos_ref[0]
    sin = sin_ref[0]
    lane = lax.broadcasted_iota(jnp.int32, (1, LANES), 1)
    first_half = (lane % HEAD_DIM) < HALF_DIM
    scale = HEAD_DIM ** -0.5
    for g in range(aw // LANES):
        cols = slice(g * LANES, (g + 1) * LANES)
        for off, ref, mul in ((0, q_ref, scale), (aw, k_ref, 1.0)):
            xg = qkv[:, off + g * LANES:off + (g + 1) * LANES]
            partner = jnp.where(first_half, pltpu.roll(xg, LANES - HALF_DIM, 1), pltpu.roll(xg, HALF_DIM, 1))
            ref[0, :, cols] = ((xg * cos + partner * sin) * mul).astype(ref.dtype)
        v_ref[0, :, cols] = qkv[:, 2 * aw + g * LANES:2 * aw + (g + 1) * LANES].astype(v_ref.dtype)


def _mix_in(h, gain, w_in, conv_w, w_conv_out, cos, sin):
    b, s, d = h.shape
    cw = conv_w.shape[1]
    aw = N_HEADS * HEAD_DIM
    tile = min(TOKEN_TILE, s)

    def rows(width):
        return pl.BlockSpec((1, tile, width), lambda i, j: (i, j, 0))

    return pl.pallas_call(
        functools.partial(_mix_in_kernel, conv_width=cw, attn_width=aw),
        grid=(b, s // tile),
        in_specs=[rows(d), _resident((1, d)), _resident(w_in.shape), _resident(conv_w.shape),
                  _resident(w_conv_out.shape), rows(LANES), rows(LANES)],
        out_specs=[rows(aw), rows(aw), rows(aw), rows(d), rows(d)],
        out_shape=[jax.ShapeDtypeStruct((b, s, aw), BF16)] * 3 + [jax.ShapeDtypeStruct((b, s, d), BF16)] * 2,
        scratch_shapes=[pltpu.VMEM((tile + SUBLANES, cw), F32)],
        compiler_params=_params("parallel", "arbitrary"),
        name="mix_in",
    )(h, gain, w_in, conv_w, w_conv_out, cos, sin)


def _moba_kernel(q_ref, k_ref, v_ref, o_ref, kmean_ref, *, n_blocks):
    qi = pl.program_id(2)
    blk = MOBA_BLOCK

    @pl.when(qi == 0)
    def _():
        for n in range(n_blocks):
            kb = k_ref[0, n * blk:(n + 1) * blk, :].astype(F32)
            kmean_ref[n:n + 1, :] = jnp.mean(kb, axis=0, keepdims=True)

    q = q_ref[0]
    lane = lax.broadcasted_iota(jnp.int32, (1, LANES), 1)
    row = lax.broadcasted_iota(jnp.int32, (blk, blk), 0)
    col = lax.broadcasted_iota(jnp.int32, (blk, blk), 1)
    causal_bias = jnp.where(col <= row, 0.0, MASKED).astype(F32)
    kmean = kmean_ref[...]
    kmean_hi = kmean.astype(BF16)
    kmean_lo = (kmean - kmean_hi.astype(F32)).astype(BF16)
    blk_idx = lax.broadcasted_iota(jnp.int32, (1, n_blocks), 1)
    ones = jnp.ones((blk, LANES), BF16)

    def head_output(hh):
        in_head = (lane // HEAD_DIM) == hh
        qh = jnp.where(in_head, q, jnp.zeros_like(q))

        gate = _dot_nt(qh, kmean_hi) + _dot_nt(qh, kmean_lo)
        gate = jnp.where(blk_idx < qi, gate, -jnp.inf)
        rank = jnp.zeros((blk, n_blocks), jnp.int32)
        for m in range(n_blocks):
            gm = gate[:, m:m + 1]
            beats = (gm > gate) | ((gm == gate) & (m < blk_idx))
            rank = rank + beats.astype(jnp.int32)
        sel_bias = jnp.where((rank < TOP_K) & (blk_idx < qi), 0.0, MASKED).astype(F32)

        def attend(start, bias, carry):
            m_old, l_old, acc_old = carry
            k_blk = k_ref[0, pl.ds(start, blk), :]
            v_blk = v_ref[0, pl.ds(start, blk), :]
            s = _dot_nt(qh, k_blk) + bias
            m_new = jnp.maximum(m_old, jnp.max(s, axis=-1, keepdims=True))
            alpha = jnp.exp(m_old - m_new)
            p = jnp.exp(s - m_new).astype(BF16)
            pv = _dot(p, jnp.concatenate([v_blk, ones], axis=1))
            return m_new, alpha * l_old + pv[:, LANES:], alpha * acc_old + pv[:, :LANES]

        carry = (jnp.full((blk, 1), MASKED, F32), jnp.zeros((blk, LANES), F32), jnp.zeros((blk, LANES), F32))
        carry = attend(pl.multiple_of(qi * blk, blk), causal_bias, carry)

        def past_block(n, carry):
            pick = jnp.where(blk_idx == n, sel_bias, 0.0)
            bias = jnp.sum(pick, axis=-1, keepdims=True)
            return attend(pl.multiple_of(n * blk, blk), bias, carry)

        _, l_fin, acc_fin = lax.fori_loop(0, qi, past_block, carry)
        return acc_fin / l_fin

    in_first = (lane // HEAD_DIM) == 0
    o_ref[0] = jnp.where(in_first, head_output(0), head_output(1)).astype(o_ref.dtype)


def _moba(q, k, v):
    b, s, aw = q.shape
    n_blocks = s // MOBA_BLOCK
    groups = aw // LANES
    qspec = pl.BlockSpec((1, MOBA_BLOCK, LANES), lambda i, g, j: (i, j, g))
    kvspec = pl.BlockSpec((1, s, LANES), lambda i, g, j: (i, 0, g))
    return pl.pallas_call(
        functools.partial(_moba_kernel, n_blocks=n_blocks),
        grid=(b, groups, n_blocks),
        in_specs=[qspec, kvspec, kvspec],
        out_specs=qspec,
        out_shape=jax.ShapeDtypeStruct((b, s, aw), BF16),
        scratch_shapes=[pltpu.VMEM((n_blocks, LANES), F32)],
        compiler_params=_params("parallel", "parallel", "arbitrary"),
        name="moba",
    )(q, k, v)


def _mix_out_kernel(h_ref, mc_ref, ga_ref, o_ref, w_attn_out_ref, w_o_ref, out_ref):
    y_attn = _dot(o_ref[...], w_attn_out_ref[...])
    merged = mc_ref[...].astype(F32) + ga_ref[...].astype(F32) * y_attn
    out_ref[...] = h_ref[...] + _dot(merged.astype(BF16), w_o_ref[...])


def _mix_out(h, mc, ga, o, w_attn_out, w_o):
    t, d = h.shape
    aw = o.shape[1]
    tile = min(TOKEN_TILE, t)
    row = pl.BlockSpec((tile, d), lambda i: (i, 0))
    return pl.pallas_call(
        _mix_out_kernel,
        grid=(t // tile,),
        in_specs=[row, row, row, pl.BlockSpec((tile, aw), lambda i: (i, 0)),
                  _resident(w_attn_out.shape), _resident(w_o.shape)],
        out_specs=row,
        out_shape=jax.ShapeDtypeStruct((t, d), F32),
        compiler_params=_params("parallel"),
        name="mix_out",
    )(h, mc, ga, o, w_attn_out, w_o)


def _ple_kernel(h_ref, p_ref, gain_ref, w_gate_ref, w_proj_ref, final_gain_ref, out_ref, *, final):
    x = h_ref[...]
    n = _rmsnorm(x, gain_ref[...]).astype(BF16)
    gate = _sigmoid(_dot(n, w_gate_ref[...]))
    y = x + gate * _dot(p_ref[...].astype(BF16), w_proj_ref[...])
    out_ref[...] = _rmsnorm(y, final_gain_ref[...]) if final else y


def _ple(h, p, gain, w_gate, w_proj, final_gain, final):
    t, d = h.shape
    pd = p.shape[1]
    tile = min(TOKEN_TILE, t)
    row = pl.BlockSpec((tile, d), lambda i: (i, 0))
    return pl.pallas_call(
        functools.partial(_ple_kernel, final=final),
        grid=(t // tile,),
        in_specs=[row, pl.BlockSpec((tile, pd), lambda i: (i, 0)), _resident((1, d)),
                  _resident(w_gate.shape), _resident(w_proj.shape), _resident((1, d))],
        out_specs=row,
        out_shape=jax.ShapeDtypeStruct((t, d), F32),
        compiler_params=_params("parallel"),
        name="ple",
    )(h, p, gain, w_gate, w_proj, final_gain)


def kernel(x, p, positions, w_in, conv_w, w_conv_out, w_attn_out, w_o, ffn1_gate, ffn1_up, ffn1_down,
           ffn2_gate, ffn2_up, ffn2_down, norm_ffn1, norm_mix, norm_ffn2, norm_ple, w_ple_gate, w_ple_proj,
           norm_final):
    b, s, d = x.shape
    depth = w_in.shape[0]
    t = b * s
    assert s % MOBA_BLOCK == 0 and s % TOKEN_TILE == 0

    freq = jnp.arange(HALF_DIM, dtype=F32) / HALF_DIM
    inv_freq = jnp.tile(ROPE_THETA ** (-freq), LANES // HALF_DIM).reshape(1, LANES)
    cos, sin = _rope_tables(positions, inv_freq)

    def gain(g):
        return g.reshape(1, d).astype(F32)

    h = x.reshape(t, d)
    for i in range(depth):
        h = _ffn(h, gain(norm_ffn1[i]), ffn1_gate[i].astype(BF16), ffn1_up[i].astype(BF16),
                 ffn1_down[i].astype(BF16))
        q, k, v, mc, ga = _mix_in(h.reshape(b, s, d), gain(norm_mix[i]), w_in[i].astype(BF16), conv_w[i],
                                  w_conv_out[i].astype(BF16), cos, sin)
        o = _moba(q, k, v)
        h = _mix_out(h, mc.reshape(t, d), ga.reshape(t, d), o.reshape(t, -1), w_attn_out[i].astype(BF16),
                     w_o[i].astype(BF16))
        h = _ffn(h, gain(norm_ffn2[i]), ffn2_gate[i].astype(BF16), ffn2_up[i].astype(BF16),
                 ffn2_down[i].astype(BF16))
        h = _ple(h, p[i].reshape(t, -1), gain(norm_ple[i]), w_ple_gate[i].astype(BF16),
                 w_ple_proj[i].astype(BF16), gain(norm_final), final=(i == depth - 1))
    return h.reshape(b, s, d)
```

```python
import functools

import jax
import jax.numpy as jnp
from jax import lax
from jax.experimental import pallas as pl
from jax.experimental.pallas import tpu as pltpu

N_HEADS = 8
HEAD_DIM = 64
HALF_DIM = HEAD_DIM // 2
MOBA_BLOCK = 256
TOP_K = 3
ROPE_THETA = 10000.0
EPS = 1e-6

LANES = 128
SUBLANES = 8
HEADS_PER_GROUP = LANES // HEAD_DIM
BF16_SUBLANES = 2 * SUBLANES
V_ROWS = HEAD_DIM + BF16_SUBLANES
VMEM_LIMIT_BYTES = 56 * 1024 * 1024
MASKED = -1e30

TOKEN_TILE = 512
FF_CHUNK = 256

F32 = jnp.float32
BF16 = jnp.bfloat16


def _dot(a, b):
    return jnp.dot(a, b, preferred_element_type=F32)


def _dot_nt(a, b):
    return lax.dot_general(a, b, (((1,), (1,)), ((), ())), preferred_element_type=F32)


def _rmsnorm(x, gain):
    ms = jnp.mean(x * x, axis=-1, keepdims=True)
    return x * lax.rsqrt(ms + EPS) * gain


def _sigmoid(x):
    return 1.0 / (1.0 + jnp.exp(-x))


def _resident(shape):
    nd = len(shape)
    return pl.BlockSpec(shape, lambda *_: (0,) * nd, pipeline_mode=pl.Buffered(1))


def _params(*semantics):
    return pltpu.CompilerParams(dimension_semantics=semantics, vmem_limit_bytes=VMEM_LIMIT_BYTES)


def _rope_table_kernel(pos_ref, inv_freq_ref, cos_ref, sin_ref):
    ang = pos_ref[0].astype(F32) * inv_freq_ref[...]
    lane = lax.broadcasted_iota(jnp.int32, (1, LANES), 1)
    first_half = (lane % HEAD_DIM) < HALF_DIM
    cos_ref[0] = jnp.cos(ang)
    sin = jnp.sin(ang)
    sin_ref[0] = jnp.where(first_half, -sin, sin)


def _rope_tables(positions, inv_freq):
    b, s = positions.shape
    tile = min(TOKEN_TILE, s)
    spec = pl.BlockSpec((1, tile, LANES), lambda i, j: (i, j, 0))
    return pl.pallas_call(
        _rope_table_kernel,
        grid=(b, s // tile),
        in_specs=[pl.BlockSpec((1, tile, 1), lambda i, j: (i, j, 0)), _resident((1, LANES))],
        out_specs=[spec, spec],
        out_shape=[jax.ShapeDtypeStruct((b, s, LANES), F32)] * 2,
        compiler_params=_params("parallel", "parallel"),
        name="rope_tables",
    )(positions.reshape(b, s, 1), inv_freq)


def _ffn_kernel(h_ref, gain_ref, wg_ref, wu_ref, wd_ref, o_ref, act_ref):
    x = h_ref[...]
    n = _rmsnorm(x, gain_ref[...]).astype(BF16)
    d_ff = wg_ref.shape[1]
    for c in range(d_ff // FF_CHUNK):
        cols = slice(c * FF_CHUNK, (c + 1) * FF_CHUNK)
        g = _dot(n, wg_ref[:, cols])
        u = _dot(n, wu_ref[:, cols])
        act_ref[:, cols] = (g * _sigmoid(g) * u).astype(BF16)
    o_ref[...] = x + 0.5 * _dot(act_ref[...], wd_ref[...])


def _ffn(h, gain, wg, wu, wd):
    t, d = h.shape
    d_ff = wg.shape[1]
    tile = min(TOKEN_TILE, t)
    row = pl.BlockSpec((tile, d), lambda i: (i, 0))
    return pl.pallas_call(
        _ffn_kernel,
        grid=(t // tile,),
        in_specs=[row, _resident((1, d)), _resident((d, d_ff)), _resident((d, d_ff)), _resident((d_ff, d))],
        out_specs=row,
        out_shape=jax.ShapeDtypeStruct((t, d), F32),
        scratch_shapes=[pltpu.VMEM((tile, d_ff), BF16)],
        compiler_params=_params("parallel"),
        name="ffn",
    )(h, gain, wg, wu, wd)


def _mix_in_kernel(h_ref, gain_ref, w_in_ref, conv_w_ref, w_conv_out_ref, cos_ref, sin_ref,
                   q_ref, k_ref, v_ref, mc_ref, ga_ref, u_ref, *, conv_width, attn_width):
    tile = h_ref.shape[1]
    cw, aw = conv_width, attn_width
    d = h_ref.shape[2]
    n = _rmsnorm(h_ref[0], gain_ref[...]).astype(BF16)

    bcx = _dot(n, w_in_ref[:, 0:3 * cw])
    b_gate, c_gate, xc = bcx[:, 0:cw], bcx[:, cw:2 * cw], bcx[:, 2 * cw:3 * cw]
    u = c_gate * xc

    @pl.when(pl.program_id(1) == 0)
    def _():
        u_ref[0:SUBLANES, :] = jnp.zeros((SUBLANES, cw), F32)

    u_ref[SUBLANES:SUBLANES + tile, :] = u
    u_prev1 = u_ref[SUBLANES - 1:SUBLANES - 1 + tile, :]
    u_prev2 = u_ref[SUBLANES - 2:SUBLANES - 2 + tile, :]
    conv_w = conv_w_ref[...]
    conv = u_prev2 * conv_w[0:1, :] + u_prev1 * conv_w[1:2, :] + u * conv_w[2:3, :]
    u_ref[0:SUBLANES, :] = u[tile - SUBLANES:tile, :]
    y_conv = _dot((b_gate * conv).astype(BF16), w_conv_out_ref[...])

    base = 3 * cw + 3 * aw
    g_conv = _dot(n, w_in_ref[:, base:base + d])
    mc_ref[0] = (_sigmoid(g_conv) * y_conv).astype(mc_ref.dtype)
    g_attn = _dot(n, w_in_ref[:, base + d:base + 2 * d])
    ga_ref[0] = _sigmoid(g_attn).astype(ga_ref.dtype)

    qkv = _dot(n, w_in_ref[:, 3 * cw:3 * cw + 3 * aw])
    cos = cos_ref[0]
    sin = sin_ref[0]
    lane = lax.broadcasted_iota(jnp.int32, (1, LANES), 1)
    first_half = (lane % HEAD_DIM) < HALF_DIM
    scale = HEAD_DIM ** -0.5
    for g in range(aw // LANES):
        cols = slice(g * LANES, (g + 1) * LANES)
        for off, ref, mul in ((0, q_ref, scale), (aw, k_ref, 1.0)):
            xg = qkv[:, off + g * LANES:off + (g + 1) * LANES]
            partner = jnp.where(first_half, pltpu.roll(xg, LANES - HALF_DIM, 1), pltpu.roll(xg, HALF_DIM, 1))
            ref[0, :, cols] = ((xg * cos + partner * sin) * mul).astype(ref.dtype)
        v_ref[0, :, cols] = qkv[:, 2 * aw + g * LANES:2 * aw + (g + 1) * LANES].astype(v_ref.dtype)


def _mix_in(h, gain, w_in, conv_w, w_conv_out, cos, sin):
    b, s, d = h.shape
    cw = conv_w.shape[1]
    aw = N_HEADS * HEAD_DIM
    tile = min(TOKEN_TILE, s)

    def rows(width):
        return pl.BlockSpec((1, tile, width), lambda i, j: (i, j, 0))

    return pl.pallas_call(
        functools.partial(_mix_in_kernel, conv_width=cw, attn_width=aw),
        grid=(b, s // tile),
        in_specs=[rows(d), _resident((1, d)), _resident(w_in.shape), _resident(conv_w.shape),
                  _resident(w_conv_out.shape), rows(LANES), rows(LANES)],
        out_specs=[rows(aw), rows(aw), rows(aw), rows(d), rows(d)],
        out_shape=[jax.ShapeDtypeStruct((b, s, aw), BF16)] * 3 + [jax.ShapeDtypeStruct((b, s, d), BF16)] * 2,
        scratch_shapes=[pltpu.VMEM((tile + SUBLANES, cw), F32)],
        compiler_params=_params("parallel", "arbitrary"),
        name="mix_in",
    )(h, gain, w_in, conv_w, w_conv_out, cos, sin)


def _moba_kernel(q_ref, k_ref, v_ref, o_ref, kmean_ref, vt_ref, qt_ref, selb_ref, s_ref, m_ref, l_ref, acc_ref,
                 *, n_blocks):
    qi = pl.program_id(1)
    blk = MOBA_BLOCK
    groups = q_ref.shape[2] // LANES

    @pl.when(qi == 0)
    def _():
        for n in range(n_blocks):
            rows = slice(n * blk, (n + 1) * blk)
            kmean_ref[n:n + 1, :] = jnp.mean(k_ref[0, rows, :].astype(F32), axis=0, keepdims=True)
            v_t = v_ref[0, rows, :].astype(F32).T.astype(BF16)
            for h in range(N_HEADS):
                vt_ref[n, h, 0:HEAD_DIM, :] = v_t[h * HEAD_DIM:(h + 1) * HEAD_DIM, :]
                vt_ref[n, h, HEAD_DIM:V_ROWS, :] = jnp.ones((V_ROWS - HEAD_DIM, blk), BF16)

    q_t = q_ref[0].astype(F32).T
    dim_row = lax.broadcasted_iota(jnp.int32, (LANES, 1), 0)
    blk_row = lax.broadcasted_iota(jnp.int32, (n_blocks, 1), 0)
    is_past = blk_row < qi
    for g in range(groups):
        kmean = kmean_ref[:, g * LANES:(g + 1) * LANES]
        kmean_hi = kmean.astype(BF16)
        kmean_lo = (kmean - kmean_hi.astype(F32)).astype(BF16)
        q_tg = q_t[g * LANES:(g + 1) * LANES, :]
        for hh in range(HEADS_PER_GROUP):
            h = g * HEADS_PER_GROUP + hh
            q_th = jnp.where((dim_row // HEAD_DIM) == hh, q_tg, 0.0).astype(BF16)
            qt_ref[h] = q_th
            gate = _dot(kmean_hi, q_th) + _dot(kmean_lo, q_th)
            gate = jnp.where(is_past, gate, -jnp.inf)
            rank = jnp.zeros((n_blocks, blk), jnp.int32)
            for m in range(n_blocks):
                gm = gate[m:m + 1, :]
                beats = (gm > gate) | ((gm == gate) & (m < blk_row))
                rank = rank + beats.astype(jnp.int32)
            selb_ref[h] = jnp.where((rank < TOP_K) & is_past, 0.0, MASKED).astype(F32)

    def attend(n, first):
        start = pl.multiple_of(n * blk, blk)
        k_blk = k_ref[0, pl.ds(start, blk), :]
        if first:
            key = lax.broadcasted_iota(jnp.int32, (blk, blk), 0)
            qry = lax.broadcasted_iota(jnp.int32, (blk, blk), 1)
            causal = key <= qry
        m_news, alphas, selected = [], [], []
        for h in range(N_HEADS):
            g = h // HEADS_PER_GROUP
            s_t = _dot(k_blk[:, g * LANES:(g + 1) * LANES], qt_ref[h])
            if first:
                s_t = jnp.where(causal, s_t, MASKED)
                m_new = jnp.max(s_t, axis=0, keepdims=True)
            else:
                bias = selb_ref[h, pl.ds(n, 1), :]
                m_old = m_ref[h]
                m_new = jnp.maximum(m_old, jnp.max(s_t, axis=0, keepdims=True) + bias)
                alphas.append(jnp.exp(m_old - m_new))
                selected.append(bias == 0.0)
            s_ref[h] = s_t
            m_ref[h] = m_new
            m_news.append(m_new)
        for h in range(N_HEADS):
            p_t = jnp.exp(s_ref[h] - m_news[h]).astype(BF16)
            pv_t = _dot(vt_ref[n, h], p_t)
            if first:
                acc_ref[h] = pv_t[:HEAD_DIM]
                l_ref[h] = pv_t[HEAD_DIM:HEAD_DIM + 1]
            else:
                pv_t = jnp.where(selected[h], pv_t, 0.0)
                acc_ref[h] = alphas[h] * acc_ref[h] + pv_t[:HEAD_DIM]
                l_ref[h] = alphas[h] * l_ref[h] + pv_t[HEAD_DIM:HEAD_DIM + 1]

    attend(qi, True)
    lax.fori_loop(0, qi, lambda n, c: (attend(n, False), c)[1], 0)

    for g in range(groups):
        o_t = jnp.concatenate(
            [acc_ref[g * HEADS_PER_GROUP + hh] / l_ref[g * HEADS_PER_GROUP + hh] for hh in range(HEADS_PER_GROUP)],
            axis=0)
        o_ref[0, :, g * LANES:(g + 1) * LANES] = o_t.T.astype(o_ref.dtype)


def _moba(q, k, v):
    b, s, aw = q.shape
    n_blocks = s // MOBA_BLOCK
    blk = MOBA_BLOCK
    qspec = pl.BlockSpec((1, blk, aw), lambda i, j: (i, j, 0))
    kvspec = pl.BlockSpec((1, s, aw), lambda i, j: (i, 0, 0))
    return pl.pallas_call(
        functools.partial(_moba_kernel, n_blocks=n_blocks),
        grid=(b, n_blocks),
        in_specs=[qspec, kvspec, kvspec],
        out_specs=qspec,
        out_shape=jax.ShapeDtypeStruct((b, s, aw), BF16),
        scratch_shapes=[
            pltpu.VMEM((n_blocks, aw), F32),
            pltpu.VMEM((n_blocks, N_HEADS, V_ROWS, blk), BF16),
            pltpu.VMEM((N_HEADS, LANES, blk), BF16),
            pltpu.VMEM((N_HEADS, n_blocks, blk), F32),
            pltpu.VMEM((N_HEADS, blk, blk), F32),
            pltpu.VMEM((N_HEADS, 1, blk), F32),
            pltpu.VMEM((N_HEADS, 1, blk), F32),
            pltpu.VMEM((N_HEADS, HEAD_DIM, blk), F32),
        ],
        compiler_params=_params("parallel", "arbitrary"),
        name="moba",
    )(q, k, v)


def _mix_out_kernel(h_ref, mc_ref, ga_ref, o_ref, w_attn_out_ref, w_o_ref, out_ref):
    y_attn = _dot(o_ref[...], w_attn_out_ref[...])
    merged = mc_ref[...].astype(F32) + ga_ref[...].astype(F32) * y_attn
    out_ref[...] = h_ref[...] + _dot(merged.astype(BF16), w_o_ref[...])


def _mix_out(h, mc, ga, o, w_attn_out, w_o):
    t, d = h.shape
    aw = o.shape[1]
    tile = min(TOKEN_TILE, t)
    row = pl.BlockSpec((tile, d), lambda i: (i, 0))
    return pl.pallas_call(
        _mix_out_kernel,
        grid=(t // tile,),
        in_specs=[row, row, row, pl.BlockSpec((tile, aw), lambda i: (i, 0)),
                  _resident(w_attn_out.shape), _resident(w_o.shape)],
        out_specs=row,
        out_shape=jax.ShapeDtypeStruct((t, d), F32),
        compiler_params=_params("parallel"),
        name="mix_out",
    )(h, mc, ga, o, w_attn_out, w_o)


def _ple_kernel(h_ref, p_ref, gain_ref, w_gate_ref, w_proj_ref, final_gain_ref, out_ref, *, final):
    x = h_ref[...]
    n = _rmsnorm(x, gain_ref[...]).astype(BF16)
    gate = _sigmoid(_dot(n, w_gate_ref[...]))
    y = x + gate * _dot(p_ref[...].astype(BF16), w_proj_ref[...])
    out_ref[...] = _rmsnorm(y, final_gain_ref[...]) if final else y


def _ple(h, p, gain, w_gate, w_proj, final_gain, final):
    t, d = h.shape
    pd = p.shape[1]
    tile = min(TOKEN_TILE, t)
    row = pl.BlockSpec((tile, d), lambda i: (i, 0))
    return pl.pallas_call(
        functools.partial(_ple_kernel, final=final),
        grid=(t // tile,),
        in_specs=[row, pl.BlockSpec((tile, pd), lambda i: (i, 0)), _resident((1, d)),
                  _resident(w_gate.shape), _resident(w_proj.shape), _resident((1, d))],
        out_specs=row,
        out_shape=jax.ShapeDtypeStruct((t, d), F32),
        compiler_params=_params("parallel"),
        name="ple",
    )(h, p, gain, w_gate, w_proj, final_gain)


def kernel(x, p, positions, w_in, conv_w, w_conv_out, w_attn_out, w_o, ffn1_gate, ffn1_up, ffn1_down,
           ffn2_gate, ffn2_up, ffn2_down, norm_ffn1, norm_mix, norm_ffn2, norm_ple, w_ple_gate, w_ple_proj,
           norm_final):
    b, s, d = x.shape
    depth = w_in.shape[0]
    t = b * s
    assert s % MOBA_BLOCK == 0 and s % TOKEN_TILE == 0

    freq = jnp.arange(HALF_DIM, dtype=F32) / HALF_DIM
    inv_freq = jnp.tile(ROPE_THETA ** (-freq), LANES // HALF_DIM).reshape(1, LANES)
    cos, sin = _rope_tables(positions, inv_freq)

    def gain(g):
        return g.reshape(1, d).astype(F32)

    h = x.reshape(t, d)
    for i in range(depth):
        h = _ffn(h, gain(norm_ffn1[i]), ffn1_gate[i].astype(BF16), ffn1_up[i].astype(BF16),
                 ffn1_down[i].astype(BF16))
        q, k, v, mc, ga = _mix_in(h.reshape(b, s, d), gain(norm_mix[i]), w_in[i].astype(BF16), conv_w[i],
                                  w_conv_out[i].astype(BF16), cos, sin)
        o = _moba(q, k, v)
        h = _mix_out(h, mc.reshape(t, d), ga.reshape(t, d), o.reshape(t, -1), w_attn_out[i].astype(BF16),
                     w_o[i].astype(BF16))
        h = _ffn(h, gain(norm_ffn2[i]), ffn2_gate[i].astype(BF16), ffn2_up[i].astype(BF16),
                 ffn2_down[i].astype(BF16))
        h = _ple(h, p[i].reshape(t, -1), gain(norm_ple[i]), w_ple_gate[i].astype(BF16),
                 w_ple_proj[i].astype(BF16), gain(norm_final), final=(i == depth - 1))
    return h.reshape(b, s, d)
```

```python
import functools

import jax
import jax.numpy as jnp
from jax import lax
from jax.experimental import pallas as pl
from jax.experimental.pallas import tpu as pltpu

N_HEADS = 8
HEAD_DIM = 64
HALF_DIM = HEAD_DIM // 2
MOBA_BLOCK = 256
TOP_K = 3
ROPE_THETA = 10000.0
EPS = 1e-6

LANES = 128
SUBLANES = 8
HEADS_PER_GROUP = LANES // HEAD_DIM
BF16_SUBLANES = 2 * SUBLANES
V_ROWS = HEAD_DIM + BF16_SUBLANES
VMEM_LIMIT_BYTES = 56 * 1024 * 1024
MASKED = -1e30
LOG2_E = 1.4426950408889634

TOKEN_TILE = 512
FF_CHUNK = 256

F32 = jnp.float32
BF16 = jnp.bfloat16


def _dot(a, b):
    return jnp.dot(a, b, preferred_element_type=F32)


def _dot_nt(a, b):
    return lax.dot_general(a, b, (((1,), (1,)), ((), ())), preferred_element_type=F32)


def _rmsnorm(x, gain):
    ms = jnp.mean(x * x, axis=-1, keepdims=True)
    return x * lax.rsqrt(ms + EPS) * gain


def _sigmoid(x):
    return 1.0 / (1.0 + jnp.exp(-x))


def _resident(shape):
    nd = len(shape)
    return pl.BlockSpec(shape, lambda *_: (0,) * nd, pipeline_mode=pl.Buffered(1))


def _params(*semantics):
    return pltpu.CompilerParams(dimension_semantics=semantics, vmem_limit_bytes=VMEM_LIMIT_BYTES)


def _rope_table_kernel(pos_ref, inv_freq_ref, cos_ref, sin_ref):
    ang = pos_ref[0].astype(F32) * inv_freq_ref[...]
    lane = lax.broadcasted_iota(jnp.int32, (1, LANES), 1)
    first_half = (lane % HEAD_DIM) < HALF_DIM
    cos_ref[0] = jnp.cos(ang)
    sin = jnp.sin(ang)
    sin_ref[0] = jnp.where(first_half, -sin, sin)


def _rope_tables(positions, inv_freq):
    b, s = positions.shape
    tile = min(TOKEN_TILE, s)
    spec = pl.BlockSpec((1, tile, LANES), lambda i, j: (i, j, 0))
    return pl.pallas_call(
        _rope_table_kernel,
        grid=(b, s // tile),
        in_specs=[pl.BlockSpec((1, tile, 1), lambda i, j: (i, j, 0)), _resident((1, LANES))],
        out_specs=[spec, spec],
        out_shape=[jax.ShapeDtypeStruct((b, s, LANES), F32)] * 2,
        compiler_params=_params("parallel", "parallel"),
        name="rope_tables",
    )(positions.reshape(b, s, 1), inv_freq)


def _ffn_kernel(h_ref, gain_ref, wg_ref, wu_ref, wd_ref, o_ref, act_ref):
    x = h_ref[...]
    n = _rmsnorm(x, gain_ref[...]).astype(BF16)
    d_ff = wg_ref.shape[1]
    for c in range(d_ff // FF_CHUNK):
        cols = slice(c * FF_CHUNK, (c + 1) * FF_CHUNK)
        g = _dot(n, wg_ref[:, cols])
        u = _dot(n, wu_ref[:, cols])
        act_ref[:, cols] = (g * _sigmoid(g) * u).astype(BF16)
    o_ref[...] = x + 0.5 * _dot(act_ref[...], wd_ref[...])


def _ffn(h, gain, wg, wu, wd):
    t, d = h.shape
    d_ff = wg.shape[1]
    tile = min(TOKEN_TILE, t)
    row = pl.BlockSpec((tile, d), lambda i: (i, 0))
    return pl.pallas_call(
        _ffn_kernel,
        grid=(t // tile,),
        in_specs=[row, _resident((1, d)), _resident((d, d_ff)), _resident((d, d_ff)), _resident((d_ff, d))],
        out_specs=row,
        out_shape=jax.ShapeDtypeStruct((t, d), F32),
        scratch_shapes=[pltpu.VMEM((tile, d_ff), BF16)],
        compiler_params=_params("parallel"),
        name="ffn",
    )(h, gain, wg, wu, wd)


def _mix_in_kernel(h_ref, gain_ref, w_in_ref, conv_w_ref, w_conv_out_ref, cos_ref, sin_ref,
                   q_ref, k_ref, v_ref, mc_ref, ga_ref, u_ref, *, conv_width, attn_width):
    tile = h_ref.shape[1]
    cw, aw = conv_width, attn_width
    d = h_ref.shape[2]
    n = _rmsnorm(h_ref[0], gain_ref[...]).astype(BF16)

    bcx = _dot(n, w_in_ref[:, 0:3 * cw])
    b_gate, c_gate, xc = bcx[:, 0:cw], bcx[:, cw:2 * cw], bcx[:, 2 * cw:3 * cw]
    u = c_gate * xc

    @pl.when(pl.program_id(1) == 0)
    def _():
        u_ref[0:SUBLANES, :] = jnp.zeros((SUBLANES, cw), F32)

    u_ref[SUBLANES:SUBLANES + tile, :] = u
    u_prev1 = u_ref[SUBLANES - 1:SUBLANES - 1 + tile, :]
    u_prev2 = u_ref[SUBLANES - 2:SUBLANES - 2 + tile, :]
    conv_w = conv_w_ref[...]
    conv = u_prev2 * conv_w[0:1, :] + u_prev1 * conv_w[1:2, :] + u * conv_w[2:3, :]
    u_ref[0:SUBLANES, :] = u[tile - SUBLANES:tile, :]
    y_conv = _dot((b_gate * conv).astype(BF16), w_conv_out_ref[...])

    base = 3 * cw + 3 * aw
    g_conv = _dot(n, w_in_ref[:, base:base + d])
    mc_ref[0] = (_sigmoid(g_conv) * y_conv).astype(mc_ref.dtype)
    g_attn = _dot(n, w_in_ref[:, base + d:base + 2 * d])
    ga_ref[0] = _sigmoid(g_attn).astype(ga_ref.dtype)

    qkv = _dot(n, w_in_ref[:, 3 * cw:3 * cw + 3 * aw])
    cos = cos_ref[0]
    sin = sin_ref[0]
    lane = lax.broadcasted_iota(jnp.int32, (1, LANES), 1)
    first_half = (lane % HEAD_DIM) < HALF_DIM
    scale = HEAD_DIM ** -0.5 * LOG2_E
    for g in range(aw // LANES):
        cols = slice(g * LANES, (g + 1) * LANES)
        for off, ref, mul in ((0, q_ref, scale), (aw, k_ref, 1.0)):
            xg = qkv[:, off + g * LANES:off + (g + 1) * LANES]
            partner = jnp.where(first_half, pltpu.roll(xg, LANES - HALF_DIM, 1), pltpu.roll(xg, HALF_DIM, 1))
            ref[0, :, cols] = ((xg * cos + partner * sin) * mul).astype(ref.dtype)
        v_ref[0, :, cols] = qkv[:, 2 * aw + g * LANES:2 * aw + (g + 1) * LANES].astype(v_ref.dtype)


def _mix_in(h, gain, w_in, conv_w, w_conv_out, cos, sin):
    b, s, d = h.shape
    cw = conv_w.shape[1]
    aw = N_HEADS * HEAD_DIM
    tile = min(TOKEN_TILE, s)

    def rows(width):
        return pl.BlockSpec((1, tile, width), lambda i, j: (i, j, 0))

    return pl.pallas_call(
        functools.partial(_mix_in_kernel, conv_width=cw, attn_width=aw),
        grid=(b, s // tile),
        in_specs=[rows(d), _resident((1, d)), _resident(w_in.shape), _resident(conv_w.shape),
                  _resident(w_conv_out.shape), rows(LANES), rows(LANES)],
        out_specs=[rows(aw), rows(aw), rows(aw), rows(d), rows(d)],
        out_shape=[jax.ShapeDtypeStruct((b, s, aw), BF16)] * 3 + [jax.ShapeDtypeStruct((b, s, d), BF16)] * 2,
        scratch_shapes=[pltpu.VMEM((tile + SUBLANES, cw), F32)],
        compiler_params=_params("parallel", "arbitrary"),
        name="mix_in",
    )(h, gain, w_in, conv_w, w_conv_out, cos, sin)


def _moba_kernel(q_ref, k_ref, v_ref, o_ref, kmean_ref, vt_ref, qt_ref, selb_ref, s0_ref, stat0_ref, s1_ref,
                 stat1_ref, m_ref, l_ref, acc_ref, *, n_blocks):
    qi = pl.program_id(1)
    blk = MOBA_BLOCK
    groups = q_ref.shape[2] // LANES

    @pl.when(qi == 0)
    def _():
        for n in range(n_blocks):
            rows = slice(n * blk, (n + 1) * blk)
            kmean_ref[n:n + 1, :] = jnp.mean(k_ref[0, rows, :].astype(F32), axis=0, keepdims=True)
            v_t = v_ref[0, rows, :].astype(F32).T.astype(BF16)
            for h in range(N_HEADS):
                vt_ref[n, h, 0:HEAD_DIM, :] = v_t[h * HEAD_DIM:(h + 1) * HEAD_DIM, :]
                vt_ref[n, h, HEAD_DIM:V_ROWS, :] = jnp.ones((V_ROWS - HEAD_DIM, blk), BF16)

    q_t = q_ref[0].astype(F32).T
    dim_row = lax.broadcasted_iota(jnp.int32, (LANES, 1), 0)
    blk_row = lax.broadcasted_iota(jnp.int32, (n_blocks, 1), 0)
    is_past = blk_row < qi
    for g in range(groups):
        kmean = kmean_ref[:, g * LANES:(g + 1) * LANES]
        kmean_hi = kmean.astype(BF16)
        kmean_lo = (kmean - kmean_hi.astype(F32)).astype(BF16)
        q_tg = q_t[g * LANES:(g + 1) * LANES, :]
        for hh in range(HEADS_PER_GROUP):
            h = g * HEADS_PER_GROUP + hh
            q_th = jnp.where((dim_row // HEAD_DIM) == hh, q_tg, 0.0).astype(BF16)
            qt_ref[h] = q_th
            gate = _dot(kmean_hi, q_th) + _dot(kmean_lo, q_th)
            gate = jnp.where(is_past, gate, -jnp.inf)
            rank = jnp.zeros((n_blocks, blk), jnp.int32)
            for m in range(n_blocks):
                gm = gate[m:m + 1, :]
                beats = (gm > gate) | ((gm == gate) & (m < blk_row))
                rank = rank + beats.astype(jnp.int32)
            selb_ref[h] = jnp.where((rank < TOP_K) & is_past, 0.0, MASKED).astype(F32)

    slots = ((s0_ref, stat0_ref), (s1_ref, stat1_ref))

    def score_stage(h, n, slot, diag):
        s_ref, stat_ref = slots[slot]
        g = h // HEADS_PER_GROUP
        start = pl.multiple_of(n * blk, blk)
        s_t = _dot(k_ref[0, pl.ds(start, blk), g * LANES:(g + 1) * LANES], qt_ref[h])
        if diag:
            key = lax.broadcasted_iota(jnp.int32, (blk, blk), 0)
            qry = lax.broadcasted_iota(jnp.int32, (blk, blk), 1)
            s_t = jnp.where(key <= qry, s_t, MASKED)
            keep = jnp.zeros((1, blk), F32)
        else:
            keep = selb_ref[h, pl.ds(n, 1), :]
        m_old = m_ref[h]
        m_new = jnp.maximum(m_old, jnp.max(s_t, axis=0, keepdims=True) + keep)
        s_ref[h] = s_t
        m_ref[h] = m_new
        stat_ref[h, 0:1, :] = m_new
        stat_ref[h, 1:2, :] = jnp.exp2(m_old - m_new)
        stat_ref[h, 2:3, :] = keep

    def value_stage(h, n, slot):
        s_ref, stat_ref = slots[slot]
        shift = stat_ref[h, 0:1, :]
        alpha = stat_ref[h, 1:2, :]
        keep = stat_ref[h, 2:3, :]
        p_t = jnp.exp2(s_ref[h] - shift).astype(BF16)
        pv_t = _dot(vt_ref[n, h], p_t)
        pv_t = jnp.where(keep == 0.0, pv_t, 0.0)
        acc_ref[h] = alpha * acc_ref[h] + pv_t[:HEAD_DIM]
        l_ref[h] = alpha * l_ref[h] + pv_t[HEAD_DIM:HEAD_DIM + 1]

    for h in range(N_HEADS):
        m_ref[h] = jnp.full((1, blk), MASKED, F32)
        l_ref[h] = jnp.zeros((1, blk), F32)
        acc_ref[h] = jnp.zeros((HEAD_DIM, blk), F32)
    for h in range(N_HEADS):
        score_stage(h, qi, 0, True)

    def step(n, slot):
        prev = jnp.where(n == 0, qi, n - 1)
        for h in range(N_HEADS):
            score_stage(h, n, 1 - slot, False)
            value_stage(h, prev, slot)

    def two_steps(t, carry):
        step(2 * t, 0)
        step(2 * t + 1, 1)
        return carry

    lax.fori_loop(0, qi // 2, two_steps, 0)

    @pl.when(qi % 2 == 1)
    def _():
        step(qi - 1, 0)
        for h in range(N_HEADS):
            value_stage(h, qi - 1, 1)

    @pl.when(qi % 2 == 0)
    def _():
        last = jnp.where(qi == 0, qi, qi - 1)
        for h in range(N_HEADS):
            value_stage(h, last, 0)

    for g in range(groups):
        o_t = jnp.concatenate(
            [acc_ref[g * HEADS_PER_GROUP + hh] / l_ref[g * HEADS_PER_GROUP + hh] for hh in range(HEADS_PER_GROUP)],
            axis=0)
        o_ref[0, :, g * LANES:(g + 1) * LANES] = o_t.T.astype(o_ref.dtype)


def _moba(q, k, v):
    b, s, aw = q.shape
    n_blocks = s // MOBA_BLOCK
    blk = MOBA_BLOCK
    qspec = pl.BlockSpec((1, blk, aw), lambda i, j: (i, j, 0))
    kvspec = pl.BlockSpec((1, s, aw), lambda i, j: (i, 0, 0))
    return pl.pallas_call(
        functools.partial(_moba_kernel, n_blocks=n_blocks),
        grid=(b, n_blocks),
        in_specs=[qspec, kvspec, kvspec],
        out_specs=qspec,
        out_shape=jax.ShapeDtypeStruct((b, s, aw), BF16),
        scratch_shapes=[
            pltpu.VMEM((n_blocks, aw), F32),
            pltpu.VMEM((n_blocks, N_HEADS, V_ROWS, blk), BF16),
            pltpu.VMEM((N_HEADS, LANES, blk), BF16),
            pltpu.VMEM((N_HEADS, n_blocks, blk), F32),
            pltpu.VMEM((N_HEADS, blk, blk), F32),
            pltpu.VMEM((N_HEADS, SUBLANES, blk), F32),
            pltpu.VMEM((N_HEADS, blk, blk), F32),
            pltpu.VMEM((N_HEADS, SUBLANES, blk), F32),
            pltpu.VMEM((N_HEADS, 1, blk), F32),
            pltpu.VMEM((N_HEADS, 1, blk), F32),
            pltpu.VMEM((N_HEADS, HEAD_DIM, blk), F32),
        ],
        compiler_params=_params("parallel", "arbitrary"),
        name="moba",
    )(q, k, v)


def _mix_out_kernel(h_ref, mc_ref, ga_ref, o_ref, w_attn_out_ref, w_o_ref, out_ref):
    y_attn = _dot(o_ref[...], w_attn_out_ref[...])
    merged = mc_ref[...].astype(F32) + ga_ref[...].astype(F32) * y_attn
    out_ref[...] = h_ref[...] + _dot(merged.astype(BF16), w_o_ref[...])


def _mix_out(h, mc, ga, o, w_attn_out, w_o):
    t, d = h.shape
    aw = o.shape[1]
    tile = min(TOKEN_TILE, t)
    row = pl.BlockSpec((tile, d), lambda i: (i, 0))
    return pl.pallas_call(
        _mix_out_kernel,
        grid=(t // tile,),
        in_specs=[row, row, row, pl.BlockSpec((tile, aw), lambda i: (i, 0)),
                  _resident(w_attn_out.shape), _resident(w_o.shape)],
        out_specs=row,
        out_shape=jax.ShapeDtypeStruct((t, d), F32),
        compiler_params=_params("parallel"),
        name="mix_out",
    )(h, mc, ga, o, w_attn_out, w_o)


def _ple_kernel(h_ref, p_ref, gain_ref, w_gate_ref, w_proj_ref, final_gain_ref, out_ref, *, final):
    x = h_ref[...]
    n = _rmsnorm(x, gain_ref[...]).astype(BF16)
    gate = _sigmoid(_dot(n, w_gate_ref[...]))
    y = x + gate * _dot(p_ref[...].astype(BF16), w_proj_ref[...])
    out_ref[...] = _rmsnorm(y, final_gain_ref[...]) if final else y


def _ple(h, p, gain, w_gate, w_proj, final_gain, final):
    t, d = h.shape
    pd = p.shape[1]
    tile = min(TOKEN_TILE, t)
    row = pl.BlockSpec((tile, d), lambda i: (i, 0))
    return pl.pallas_call(
        functools.partial(_ple_kernel, final=final),
        grid=(t // tile,),
        in_specs=[row, pl.BlockSpec((tile, pd), lambda i: (i, 0)), _resident((1, d)),
                  _resident(w_gate.shape), _resident(w_proj.shape), _resident((1, d))],
        out_specs=row,
        out_shape=jax.ShapeDtypeStruct((t, d), F32),
        compiler_params=_params("parallel"),
        name="ple",
    )(h, p, gain, w_gate, w_proj, final_gain)


def kernel(x, p, positions, w_in, conv_w, w_conv_out, w_attn_out, w_o, ffn1_gate, ffn1_up, ffn1_down,
           ffn2_gate, ffn2_up, ffn2_down, norm_ffn1, norm_mix, norm_ffn2, norm_ple, w_ple_gate, w_ple_proj,
           norm_final):
    b, s, d = x.shape
    depth = w_in.shape[0]
    t = b * s
    assert s % MOBA_BLOCK == 0 and s % TOKEN_TILE == 0

    freq = jnp.arange(HALF_DIM, dtype=F32) / HALF_DIM
    inv_freq = jnp.tile(ROPE_THETA ** (-freq), LANES // HALF_DIM).reshape(1, LANES)
    cos, sin = _rope_tables(positions, inv_freq)

    def gain(g):
        return g.reshape(1, d).astype(F32)

    h = x.reshape(t, d)
    for i in range(depth):
        h = _ffn(h, gain(norm_ffn1[i]), ffn1_gate[i].astype(BF16), ffn1_up[i].astype(BF16),
                 ffn1_down[i].astype(BF16))
        q, k, v, mc, ga = _mix_in(h.reshape(b, s, d), gain(norm_mix[i]), w_in[i].astype(BF16), conv_w[i],
                                  w_conv_out[i].astype(BF16), cos, sin)
        o = _moba(q, k, v)
        h = _mix_out(h, mc.reshape(t, d), ga.reshape(t, d), o.reshape(t, -1), w_attn_out[i].astype(BF16),
                     w_o[i].astype(BF16))
        h = _ffn(h, gain(norm_ffn2[i]), ffn2_gate[i].astype(BF16), ffn2_up[i].astype(BF16),
                 ffn2_down[i].astype(BF16))
        h = _ple(h, p[i].reshape(t, -1), gain(norm_ple[i]), w_ple_gate[i].astype(BF16),
                 w_ple_proj[i].astype(BF16), gain(norm_final), final=(i == depth - 1))
    return h.reshape(b, s, d)
```

```python
import functools

import jax
import jax.numpy as jnp
from jax import lax
from jax.experimental import pallas as pl
from jax.experimental.pallas import tpu as pltpu

N_HEADS = 8
HEAD_DIM = 64
HALF_DIM = HEAD_DIM // 2
MOBA_BLOCK = 256
TOP_K = 3
ROPE_THETA = 10000.0
EPS = 1e-6

LANES = 128
SUBLANES = 8
HEADS_PER_GROUP = LANES // HEAD_DIM
BF16_SUBLANES = 2 * SUBLANES
V_ROWS = HEAD_DIM + BF16_SUBLANES
VMEM_LIMIT_BYTES = 56 * 1024 * 1024
MASKED = -1e30
LOG2_E = 1.4426950408889634

TOKEN_TILE = 512
FF_CHUNK = 256
OUT_CHUNK = 256
PROJ_CHUNK = 512

F32 = jnp.float32
BF16 = jnp.bfloat16


def _dot(a, b):
    return jnp.dot(a, b, preferred_element_type=F32)


def _dot_nt(a, b):
    return lax.dot_general(a, b, (((1,), (1,)), ((), ())), preferred_element_type=F32)


def _rmsnorm(x, gain):
    ms = jnp.mean(x * x, axis=-1, keepdims=True)
    return x * lax.rsqrt(ms + EPS) * gain


def _sigmoid(x):
    return 1.0 / (1.0 + jnp.exp(-x))


def _resident(shape):
    nd = len(shape)
    return pl.BlockSpec(shape, lambda *_: (0,) * nd, pipeline_mode=pl.Buffered(1))


def _layer(stacked_shape, layer):
    nd = len(stacked_shape) - 1
    return pl.BlockSpec((None,) + tuple(stacked_shape[1:]), lambda *_: (layer,) + (0,) * nd,
                        pipeline_mode=pl.Buffered(1))


def _bf16(ref, rows=slice(None), cols=slice(None)):
    return ref[rows, cols].astype(BF16)


def _params(*semantics):
    return pltpu.CompilerParams(dimension_semantics=semantics, vmem_limit_bytes=VMEM_LIMIT_BYTES)


def _rope_table_kernel(pos_ref, inv_freq_ref, cos_ref, sin_ref):
    ang = pos_ref[0].astype(F32) * inv_freq_ref[...]
    lane = lax.broadcasted_iota(jnp.int32, (1, LANES), 1)
    first_half = (lane % HEAD_DIM) < HALF_DIM
    cos_ref[0] = jnp.cos(ang)
    sin = jnp.sin(ang)
    sin_ref[0] = jnp.where(first_half, -sin, sin)


def _rope_tables(positions, inv_freq):
    b, s = positions.shape
    tile = min(TOKEN_TILE, s)
    spec = pl.BlockSpec((1, tile, LANES), lambda i, j: (i, j, 0))
    return pl.pallas_call(
        _rope_table_kernel,
        grid=(b, s // tile),
        in_specs=[pl.BlockSpec((1, tile, 1), lambda i, j: (i, j, 0)), _resident((1, LANES))],
        out_specs=[spec, spec],
        out_shape=[jax.ShapeDtypeStruct((b, s, LANES), F32)] * 2,
        compiler_params=_params("parallel", "parallel"),
        name="rope_tables",
    )(positions.reshape(b, s, 1), inv_freq)


def _ffn_rows(h_ref, gain_ref, wg_ref, wu_ref, wd_ref, o_ref, act_ref):
    x = h_ref[...]
    n = _rmsnorm(x, gain_ref[...]).astype(BF16)
    d_ff = wg_ref.shape[1]
    for c in range(d_ff // FF_CHUNK):
        cols = slice(c * FF_CHUNK, (c + 1) * FF_CHUNK)
        g = _dot(n, _bf16(wg_ref, cols=cols))
        u = _dot(n, _bf16(wu_ref, cols=cols))
        act_ref[:, cols] = (g * _sigmoid(g) * u).astype(BF16)
    for c in range(x.shape[1] // OUT_CHUNK):
        cols = slice(c * OUT_CHUNK, (c + 1) * OUT_CHUNK)
        o_ref[:, cols] = x[:, cols] + 0.5 * _dot(act_ref[...], _bf16(wd_ref, cols=cols))


def _ffn_kernel(h_ref, gain_ref, wg_ref, wu_ref, wd_ref, o_ref, act_ref):
    _ffn_rows(h_ref, gain_ref, wg_ref, wu_ref, wd_ref, o_ref, act_ref)


def _ffn_ple_kernel(h_ref, gain_ref, wg_ref, wu_ref, wd_ref, p_ref, ple_gain_ref, w_gate_ref, w_proj_ref,
                    final_gain_ref, o_ref, act_ref, *, final):
    _ffn_rows(h_ref, gain_ref, wg_ref, wu_ref, wd_ref, o_ref, act_ref)
    y = o_ref[...]
    n = _rmsnorm(y, ple_gain_ref[...]).astype(BF16)
    gate = _sigmoid(_dot(n, _bf16(w_gate_ref)))
    y = y + gate * _dot(p_ref[...].astype(BF16), _bf16(w_proj_ref))
    o_ref[...] = _rmsnorm(y, final_gain_ref[...]) if final else y


def _ffn(h, gains, wg, wu, wd, layer, ple=None):
    t, d = h.shape
    d_ff = wg.shape[2]
    tile = min(TOKEN_TILE, t)
    row = pl.BlockSpec((tile, d), lambda i: (i, 0))
    in_specs = [row, _layer(gains.shape, layer), _layer(wg.shape, layer), _layer(wu.shape, layer),
                _layer(wd.shape, layer)]
    args = [h, gains, wg, wu, wd]
    body = _ffn_kernel
    if ple is not None:
        p, ple_gains, w_gate, w_proj, final_gain, final = ple
        in_specs += [pl.BlockSpec((None, tile, p.shape[2]), lambda i: (layer, i, 0)),
                     _layer(ple_gains.shape, layer), _layer(w_gate.shape, layer), _layer(w_proj.shape, layer),
                     _resident(final_gain.shape)]
        args += [p, ple_gains, w_gate, w_proj, final_gain]
        body = functools.partial(_ffn_ple_kernel, final=final)
    return pl.pallas_call(
        body,
        grid=(t // tile,),
        in_specs=in_specs,
        out_specs=row,
        out_shape=jax.ShapeDtypeStruct((t, d), F32),
        scratch_shapes=[pltpu.VMEM((tile, d_ff), BF16)],
        compiler_params=_params("parallel"),
        name="ffn_ple" if ple is not None else "ffn",
    )(*args)


def _mix_in_kernel(h_ref, gain_ref, w_in_ref, conv_w_ref, w_conv_out_ref, cos_ref, sin_ref,
                   q_ref, k_ref, v_ref, mc_ref, ga_ref, u_ref, *, conv_width, attn_width):
    tile = h_ref.shape[1]
    cw, aw = conv_width, attn_width
    d = h_ref.shape[2]
    @pl.when(pl.program_id(1) == 0)
    def _():
        u_ref[0:SUBLANES, :] = jnp.zeros((SUBLANES, cw), F32)

    n = _rmsnorm(h_ref[0], gain_ref[...]).astype(BF16)

    def proj(col0, width):
        return _dot(n, _bf16(w_in_ref, cols=slice(col0, col0 + width)))

    b_gate, c_gate, xc = proj(0, cw), proj(cw, cw), proj(2 * cw, cw)
    u = c_gate * xc
    u_ref[SUBLANES:SUBLANES + tile, :] = u

    cos = cos_ref[0]
    sin = sin_ref[0]
    lane = lax.broadcasted_iota(jnp.int32, (1, LANES), 1)
    first_half = (lane % HEAD_DIM) < HALF_DIM
    scale = HEAD_DIM ** -0.5 * LOG2_E
    for off, ref, mul in ((3 * cw, q_ref, scale), (3 * cw + aw, k_ref, 1.0)):
        xr = proj(off, aw)
        for g in range(aw // LANES):
            cols = slice(g * LANES, (g + 1) * LANES)
            xg = xr[:, cols]
            partner = jnp.where(first_half, pltpu.roll(xg, LANES - HALF_DIM, 1), pltpu.roll(xg, HALF_DIM, 1))
            ref[0, :, cols] = ((xg * cos + partner * sin) * mul).astype(ref.dtype)
    v_ref[0] = proj(3 * cw + 2 * aw, aw).astype(v_ref.dtype)
    base = 3 * cw + 3 * aw
    for c in range(d // PROJ_CHUNK):
        cols = slice(c * PROJ_CHUNK, (c + 1) * PROJ_CHUNK)
        ga_ref[0, :, cols] = _sigmoid(proj(base + d + c * PROJ_CHUNK, PROJ_CHUNK)).astype(ga_ref.dtype)
    conv_gate = [_sigmoid(proj(base + c * PROJ_CHUNK, PROJ_CHUNK)) for c in range(d // PROJ_CHUNK)]

    u_prev1 = u_ref[SUBLANES - 1:SUBLANES - 1 + tile, :]
    u_prev2 = u_ref[SUBLANES - 2:SUBLANES - 2 + tile, :]
    conv_w = conv_w_ref[...]
    conv = u_prev2 * conv_w[0:1, :] + u_prev1 * conv_w[1:2, :] + u * conv_w[2:3, :]
    u_ref[0:SUBLANES, :] = u[tile - SUBLANES:tile, :]
    gated = (b_gate * conv).astype(BF16)
    for c in range(d // PROJ_CHUNK):
        cols = slice(c * PROJ_CHUNK, (c + 1) * PROJ_CHUNK)
        y_conv = _dot(gated, _bf16(w_conv_out_ref, cols=cols))
        mc_ref[0, :, cols] = (conv_gate[c] * y_conv).astype(mc_ref.dtype)


def _mix_in(h, gains, w_in, conv_w, w_conv_out, cos, sin, layer):
    b, s, d = h.shape
    cw = conv_w.shape[2]
    aw = N_HEADS * HEAD_DIM
    tile = min(TOKEN_TILE, s)

    def rows(width):
        return pl.BlockSpec((1, tile, width), lambda i, j: (i, j, 0))

    return pl.pallas_call(
        functools.partial(_mix_in_kernel, conv_width=cw, attn_width=aw),
        grid=(b, s // tile),
        in_specs=[rows(d), _layer(gains.shape, layer), _layer(w_in.shape, layer), _layer(conv_w.shape, layer),
                  _layer(w_conv_out.shape, layer), rows(LANES), rows(LANES)],
        out_specs=[rows(aw), rows(aw), rows(aw), rows(d), rows(d)],
        out_shape=[jax.ShapeDtypeStruct((b, s, aw), BF16)] * 3 + [jax.ShapeDtypeStruct((b, s, d), BF16)] * 2,
        scratch_shapes=[pltpu.VMEM((tile + SUBLANES, cw), F32)],
        compiler_params=_params("parallel", "arbitrary"),
        name="mix_in",
    )(h, gains, w_in, conv_w, w_conv_out, cos, sin)


def _moba_kernel(q_ref, k_ref, v_ref, o_ref, kmean_ref, vt_ref, qt_ref, selb_ref, s0_ref, stat0_ref, s1_ref,
                 stat1_ref, m_ref, l_ref, acc_ref, *, n_blocks):
    qi = pl.program_id(1)
    blk = MOBA_BLOCK
    groups = q_ref.shape[2] // LANES

    @pl.when(qi == 0)
    def _():
        for n in range(n_blocks):
            rows = slice(n * blk, (n + 1) * blk)
            kmean_ref[n:n + 1, :] = jnp.mean(k_ref[0, rows, :].astype(F32), axis=0, keepdims=True)
            v_t = v_ref[0, rows, :].astype(F32).T.astype(BF16)
            for h in range(N_HEADS):
                vt_ref[n, h, 0:HEAD_DIM, :] = v_t[h * HEAD_DIM:(h + 1) * HEAD_DIM, :]
                vt_ref[n, h, HEAD_DIM:V_ROWS, :] = jnp.ones((V_ROWS - HEAD_DIM, blk), BF16)

    q_t = q_ref[0].astype(F32).T
    dim_row = lax.broadcasted_iota(jnp.int32, (LANES, 1), 0)
    blk_row = lax.broadcasted_iota(jnp.int32, (n_blocks, 1), 0)
    is_past = blk_row < qi
    for g in range(groups):
        kmean = kmean_ref[:, g * LANES:(g + 1) * LANES]
        kmean_hi = kmean.astype(BF16)
        kmean_lo = (kmean - kmean_hi.astype(F32)).astype(BF16)
        q_tg = q_t[g * LANES:(g + 1) * LANES, :]
        for hh in range(HEADS_PER_GROUP):
            h = g * HEADS_PER_GROUP + hh
            q_th = jnp.where((dim_row // HEAD_DIM) == hh, q_tg, 0.0).astype(BF16)
            qt_ref[h] = q_th
            gate = _dot(kmean_hi, q_th) + _dot(kmean_lo, q_th)
            gate = jnp.where(is_past, gate, -jnp.inf)
            rank = jnp.zeros((n_blocks, blk), jnp.int32)
            for m in range(n_blocks):
                gm = gate[m:m + 1, :]
                beats = (gm > gate) | ((gm == gate) & (m < blk_row))
                rank = rank + beats.astype(jnp.int32)
            selb_ref[h] = jnp.where((rank < TOP_K) & is_past, 0.0, MASKED).astype(F32)

    slots = ((s0_ref, stat0_ref), (s1_ref, stat1_ref))

    def score_stage(h, n, slot, diag):
        s_ref, stat_ref = slots[slot]
        g = h // HEADS_PER_GROUP
        start = pl.multiple_of(n * blk, blk)
        s_t = _dot(k_ref[0, pl.ds(start, blk), g * LANES:(g + 1) * LANES], qt_ref[h])
        if diag:
            key = lax.broadcasted_iota(jnp.int32, (blk, blk), 0)
            qry = lax.broadcasted_iota(jnp.int32, (blk, blk), 1)
            s_t = jnp.where(key <= qry, s_t, MASKED)
            keep = jnp.zeros((1, blk), F32)
        else:
            keep = selb_ref[h, pl.ds(n, 1), :]
        m_old = m_ref[h]
        m_new = jnp.maximum(m_old, jnp.max(s_t, axis=0, keepdims=True) + keep)
        s_ref[h] = s_t
        m_ref[h] = m_new
        stat_ref[h, 0:1, :] = m_new
        stat_ref[h, 1:2, :] = jnp.exp2(m_old - m_new)
        stat_ref[h, 2:3, :] = keep

    def value_stage(h, n, slot):
        s_ref, stat_ref = slots[slot]
        shift = stat_ref[h, 0:1, :]
        alpha = stat_ref[h, 1:2, :]
        keep = stat_ref[h, 2:3, :]
        p_t = jnp.exp2(s_ref[h] - shift).astype(BF16)
        pv_t = _dot(vt_ref[n, h], p_t)
        pv_t = jnp.where(keep == 0.0, pv_t, 0.0)
        acc_ref[h] = alpha * acc_ref[h] + pv_t[:HEAD_DIM]
        l_ref[h] = alpha * l_ref[h] + pv_t[HEAD_DIM:HEAD_DIM + 1]

    for h in range(N_HEADS):
        m_ref[h] = jnp.full((1, blk), MASKED, F32)
        l_ref[h] = jnp.zeros((1, blk), F32)
        acc_ref[h] = jnp.zeros((HEAD_DIM, blk), F32)
    for h in range(N_HEADS):
        score_stage(h, qi, 0, True)

    def step(n, slot):
        prev = jnp.where(n == 0, qi, n - 1)
        for h in range(N_HEADS):
            score_stage(h, n, 1 - slot, False)
            value_stage(h, prev, slot)

    def two_steps(t, carry):
        step(2 * t, 0)
        step(2 * t + 1, 1)
        return carry

    lax.fori_loop(0, qi // 2, two_steps, 0)

    @pl.when(qi % 2 == 1)
    def _():
        step(qi - 1, 0)
        for h in range(N_HEADS):
            value_stage(h, qi - 1, 1)

    @pl.when(qi % 2 == 0)
    def _():
        last = jnp.where(qi == 0, qi, qi - 1)
        for h in range(N_HEADS):
            value_stage(h, last, 0)

    for g in range(groups):
        o_t = jnp.concatenate(
            [acc_ref[g * HEADS_PER_GROUP + hh] / l_ref[g * HEADS_PER_GROUP + hh] for hh in range(HEADS_PER_GROUP)],
            axis=0)
        o_ref[0, :, g * LANES:(g + 1) * LANES] = o_t.T.astype(o_ref.dtype)


def _moba(q, k, v):
    b, s, aw = q.shape
    n_blocks = s // MOBA_BLOCK
    blk = MOBA_BLOCK
    qspec = pl.BlockSpec((1, blk, aw), lambda i, j: (i, j, 0))
    kvspec = pl.BlockSpec((1, s, aw), lambda i, j: (i, 0, 0))
    return pl.pallas_call(
        functools.partial(_moba_kernel, n_blocks=n_blocks),
        grid=(b, n_blocks),
        in_specs=[qspec, kvspec, kvspec],
        out_specs=qspec,
        out_shape=jax.ShapeDtypeStruct((b, s, aw), BF16),
        scratch_shapes=[
            pltpu.VMEM((n_blocks, aw), F32),
            pltpu.VMEM((n_blocks, N_HEADS, V_ROWS, blk), BF16),
            pltpu.VMEM((N_HEADS, LANES, blk), BF16),
            pltpu.VMEM((N_HEADS, n_blocks, blk), F32),
            pltpu.VMEM((N_HEADS, blk, blk), F32),
            pltpu.VMEM((N_HEADS, SUBLANES, blk), F32),
            pltpu.VMEM((N_HEADS, blk, blk), F32),
            pltpu.VMEM((N_HEADS, SUBLANES, blk), F32),
            pltpu.VMEM((N_HEADS, 1, blk), F32),
            pltpu.VMEM((N_HEADS, 1, blk), F32),
            pltpu.VMEM((N_HEADS, HEAD_DIM, blk), F32),
        ],
        compiler_params=_params("parallel", "arbitrary"),
        name="moba",
    )(q, k, v)


def _mix_out_kernel(h_ref, mc_ref, ga_ref, o_ref, w_attn_out_ref, w_o_ref, out_ref):
    y_attn = _dot(o_ref[...], _bf16(w_attn_out_ref))
    merged = mc_ref[...].astype(F32) + ga_ref[...].astype(F32) * y_attn
    out_ref[...] = h_ref[...] + _dot(merged.astype(BF16), _bf16(w_o_ref))


def _mix_out(h, mc, ga, o, w_attn_out, w_o, layer):
    t, d = h.shape
    aw = o.shape[1]
    tile = min(TOKEN_TILE, t)
    row = pl.BlockSpec((tile, d), lambda i: (i, 0))
    return pl.pallas_call(
        _mix_out_kernel,
        grid=(t // tile,),
        in_specs=[row, row, row, pl.BlockSpec((tile, aw), lambda i: (i, 0)),
                  _layer(w_attn_out.shape, layer), _layer(w_o.shape, layer)],
        out_specs=row,
        out_shape=jax.ShapeDtypeStruct((t, d), F32),
        compiler_params=_params("parallel"),
        name="mix_out",
    )(h, mc, ga, o, w_attn_out, w_o)


def kernel(x, p, positions, w_in, conv_w, w_conv_out, w_attn_out, w_o, ffn1_gate, ffn1_up, ffn1_down,
           ffn2_gate, ffn2_up, ffn2_down, norm_ffn1, norm_mix, norm_ffn2, norm_ple, w_ple_gate, w_ple_proj,
           norm_final):
    b, s, d = x.shape
    depth = w_in.shape[0]
    t = b * s
    assert s % MOBA_BLOCK == 0 and s % TOKEN_TILE == 0

    freq = jnp.arange(HALF_DIM, dtype=F32) / HALF_DIM
    inv_freq = jnp.tile(ROPE_THETA ** (-freq), LANES // HALF_DIM).reshape(1, LANES)
    cos, sin = _rope_tables(positions, inv_freq)

    def gains(g):
        return g.reshape(depth, 1, d)

    p_rows = p.reshape(depth, t, p.shape[-1])
    h = x.reshape(t, d)
    for i in range(depth):
        h = _ffn(h, gains(norm_ffn1), ffn1_gate, ffn1_up, ffn1_down, i)
        q, k, v, mc, ga = _mix_in(h.reshape(b, s, d), gains(norm_mix), w_in, conv_w, w_conv_out, cos, sin, i)
        o = _moba(q, k, v)
        h = _mix_out(h, mc.reshape(t, d), ga.reshape(t, d), o.reshape(t, -1), w_attn_out, w_o, i)
        ple = (p_rows, gains(norm_ple), w_ple_gate, w_ple_proj, norm_final.reshape(1, d), i == depth - 1)
        h = _ffn(h, gains(norm_ffn2), ffn2_gate, ffn2_up, ffn2_down, i, ple=ple)
    return h.reshape(b, s, d)
```

```python
import functools

import jax
import jax.numpy as jnp
from jax import lax
from jax.experimental import pallas as pl
from jax.experimental.pallas import tpu as pltpu

N_HEADS = 8
HEAD_DIM = 64
HALF_DIM = HEAD_DIM // 2
MOBA_BLOCK = 256
TOP_K = 3
ROPE_THETA = 10000.0
EPS = 1e-6

LANES = 128
SUBLANES = 8
HEADS_PER_GROUP = LANES // HEAD_DIM
BF16_SUBLANES = 2 * SUBLANES
V_ROWS = HEAD_DIM + BF16_SUBLANES
VMEM_LIMIT_BYTES = 56 * 1024 * 1024
MASKED = -1e30
LOG2_E = 1.4426950408889634

TOKEN_TILE = 512
STAGE_ROWS = 256
FF_CHUNK = 256
OUT_CHUNK = 256
PROJ_CHUNK = 512

F32 = jnp.float32
BF16 = jnp.bfloat16


def _dot(a, b):
    return jnp.dot(a, b, preferred_element_type=F32)


def _dot_nt(a, b):
    return lax.dot_general(a, b, (((1,), (1,)), ((), ())), preferred_element_type=F32)


def _rmsnorm(x, gain):
    ms = jnp.mean(x * x, axis=-1, keepdims=True)
    return x * lax.rsqrt(ms + EPS) * gain


def _sigmoid(x):
    return 1.0 / (1.0 + jnp.exp(-x))


def _resident(shape):
    nd = len(shape)
    return pl.BlockSpec(shape, lambda *_: (0,) * nd, pipeline_mode=pl.Buffered(1))


def _layer(stacked_shape, layer):
    nd = len(stacked_shape) - 1
    return pl.BlockSpec((None,) + tuple(stacked_shape[1:]), lambda *_: (layer,) + (0,) * nd,
                        pipeline_mode=pl.Buffered(1))


def _bf16(ref, rows=slice(None), cols=slice(None)):
    return ref[rows, cols].astype(BF16)


def _params(*semantics):
    return pltpu.CompilerParams(dimension_semantics=semantics, vmem_limit_bytes=VMEM_LIMIT_BYTES)


def _rope_table_kernel(pos_ref, inv_freq_ref, cos_ref, sin_ref):
    ang = pos_ref[0].astype(F32) * inv_freq_ref[...]
    lane = lax.broadcasted_iota(jnp.int32, (1, LANES), 1)
    first_half = (lane % HEAD_DIM) < HALF_DIM
    cos_ref[0] = jnp.cos(ang)
    sin = jnp.sin(ang)
    sin_ref[0] = jnp.where(first_half, -sin, sin)


def _rope_tables(positions, inv_freq):
    b, s = positions.shape
    tile = min(TOKEN_TILE, s)
    spec = pl.BlockSpec((1, tile, LANES), lambda i, j: (i, j, 0))
    return pl.pallas_call(
        _rope_table_kernel,
        grid=(b, s // tile),
        in_specs=[pl.BlockSpec((1, tile, 1), lambda i, j: (i, j, 0)), _resident((1, LANES))],
        out_specs=[spec, spec],
        out_shape=[jax.ShapeDtypeStruct((b, s, LANES), F32)] * 2,
        compiler_params=_params("parallel", "parallel"),
        name="rope_tables",
    )(positions.reshape(b, s, 1), inv_freq)


def _ffn_rows(h_ref, gain_ref, wg_ref, wu_ref, wd_ref, o_ref, act_ref):
    x = h_ref[...]
    n = _rmsnorm(x, gain_ref[...]).astype(BF16)
    d_ff = wg_ref.shape[1]
    for c in range(d_ff // FF_CHUNK):
        cols = slice(c * FF_CHUNK, (c + 1) * FF_CHUNK)
        g = _dot(n, _bf16(wg_ref, cols=cols))
        u = _dot(n, _bf16(wu_ref, cols=cols))
        act_ref[:, cols] = (g * _sigmoid(g) * u).astype(BF16)
    for c in range(x.shape[1] // OUT_CHUNK):
        cols = slice(c * OUT_CHUNK, (c + 1) * OUT_CHUNK)
        o_ref[:, cols] = x[:, cols] + 0.5 * _dot(act_ref[...], _bf16(wd_ref, cols=cols))


def _ffn(h, gains, wg, wu, wd, layer):
    t, d = h.shape
    d_ff = wg.shape[2]
    tile = min(TOKEN_TILE, t)
    row = pl.BlockSpec((tile, d), lambda i: (i, 0))
    return pl.pallas_call(
        _ffn_rows,
        grid=(t // tile,),
        in_specs=[row, _layer(gains.shape, layer), _layer(wg.shape, layer), _layer(wu.shape, layer),
                  _layer(wd.shape, layer)],
        out_specs=row,
        out_shape=jax.ShapeDtypeStruct((t, d), F32),
        scratch_shapes=[pltpu.VMEM((tile, d_ff), BF16)],
        compiler_params=_params("parallel"),
        name="ffn",
    )(h, gains, wg, wu, wd)


def _col_windows(src_hbm, layer, dst_ref, width):
    return [(src_hbm.at[layer, :, c:c + width], dst_ref.at[:, c:c + width])
            for c in range(0, dst_ref.shape[1], width)]


def _row_windows(src_hbm, layer, dst_ref, height):
    return [(src_hbm.at[layer, r:r + height, :], dst_ref.at[r:r + height, :])
            for r in range(0, dst_ref.shape[0], height)]


def _stage_params(jobs, ring_ref, sem_ref):
    def copy(i):
        return pltpu.make_async_copy(jobs[i][0], ring_ref.at[i % 2], sem_ref.at[i % 2])

    copy(0).start()
    for i, (_, dst) in enumerate(jobs):
        if i + 1 < len(jobs):
            copy(i + 1).start()
        copy(i).wait()
        dst[...] = ring_ref[i % 2].astype(BF16)


def _mix_ffn_ple_kernel(h_ref, mc_ref, ga_ref, attn_ref, p_ref, gain_ref, ple_gain_ref, final_gain_ref,
                        w_attn_out_hbm, w_o_hbm, wg_hbm, wu_hbm, wd_hbm, w_gate_hbm, w_proj_hbm, o_ref,
                        w_attn_out_ref, w_o_ref, wg_ref, wu_ref, wd_ref, w_gate_ref, w_proj_ref,
                        col_ring, row_ring, col_sem, row_sem, act_ref, x_ref, *, layer, final):
    @pl.when(pl.program_id(0) == 0)
    def _():
        row_jobs = (_row_windows(w_attn_out_hbm, layer, w_attn_out_ref, STAGE_ROWS)
                    + _row_windows(wd_hbm, layer, wd_ref, STAGE_ROWS)
                    + _row_windows(w_proj_hbm, layer, w_proj_ref, STAGE_ROWS))
        col_jobs = _col_windows(w_o_hbm, layer, w_o_ref, FF_CHUNK)
        for gate_job, up_job in zip(_col_windows(wg_hbm, layer, wg_ref, FF_CHUNK),
                                    _col_windows(wu_hbm, layer, wu_ref, FF_CHUNK)):
            col_jobs += [gate_job, up_job]
        col_jobs += _col_windows(w_gate_hbm, layer, w_gate_ref, FF_CHUNK)
        _stage_params(row_jobs, row_ring, row_sem)
        _stage_params(col_jobs, col_ring, col_sem)

    y_attn = _dot(attn_ref[...], w_attn_out_ref[...])
    merged = mc_ref[...].astype(F32) + ga_ref[...].astype(F32) * y_attn
    x_ref[...] = h_ref[...] + _dot(merged.astype(BF16), w_o_ref[...])
    _ffn_rows(x_ref, gain_ref, wg_ref, wu_ref, wd_ref, o_ref, act_ref)
    y = o_ref[...]
    n = _rmsnorm(y, ple_gain_ref[...]).astype(BF16)
    gate = _sigmoid(_dot(n, w_gate_ref[...]))
    y = y + gate * _dot(p_ref[...].astype(BF16), w_proj_ref[...])
    o_ref[...] = _rmsnorm(y, final_gain_ref[...]) if final else y


def _mix_ffn_ple(h, mc, ga, attn, p, gains, ple_gains, final_gain, w_attn_out, w_o, wg, wu, wd, w_gate, w_proj,
                 layer, final):
    t, d = h.shape
    d_ff = wg.shape[2]
    aw = attn.shape[1]
    pd = p.shape[2]
    tile = min(TOKEN_TILE, t)

    def rows(width):
        return pl.BlockSpec((tile, width), lambda i: (i, 0))

    hbm = pl.BlockSpec(memory_space=pl.ANY)
    return pl.pallas_call(
        functools.partial(_mix_ffn_ple_kernel, layer=layer, final=final),
        grid=(t // tile,),
        in_specs=[rows(d), rows(d), rows(d), rows(aw), pl.BlockSpec((None, tile, pd), lambda i: (layer, i, 0)),
                  _layer(gains.shape, layer), _layer(ple_gains.shape, layer), _resident(final_gain.shape)]
                 + [hbm] * 7,
        out_specs=rows(d),
        out_shape=jax.ShapeDtypeStruct((t, d), F32),
        scratch_shapes=[
            pltpu.VMEM((aw, d), BF16), pltpu.VMEM((d, d), BF16), pltpu.VMEM((d, d_ff), BF16),
            pltpu.VMEM((d, d_ff), BF16), pltpu.VMEM((d_ff, d), BF16), pltpu.VMEM((d, d), BF16),
            pltpu.VMEM((pd, d), BF16),
            pltpu.VMEM((2, d, FF_CHUNK), F32), pltpu.VMEM((2, STAGE_ROWS, d), F32),
            pltpu.SemaphoreType.DMA((2,)), pltpu.SemaphoreType.DMA((2,)),
            pltpu.VMEM((tile, d_ff), BF16), pltpu.VMEM((tile, d), F32),
        ],
        compiler_params=_params("arbitrary"),
        name="mix_ffn_ple",
    )(h, mc, ga, attn, p, gains, ple_gains, final_gain, w_attn_out, w_o, wg, wu, wd, w_gate, w_proj)


def _mix_in_kernel(h_ref, gain_ref, w_in_ref, conv_w_ref, w_conv_out_ref, cos_ref, sin_ref,
                   q_ref, k_ref, v_ref, mc_ref, ga_ref, u_ref, *, conv_width, attn_width):
    tile = h_ref.shape[1]
    cw, aw = conv_width, attn_width
    d = h_ref.shape[2]
    @pl.when(pl.program_id(1) == 0)
    def _():
        u_ref[0:SUBLANES, :] = jnp.zeros((SUBLANES, cw), F32)

    n = _rmsnorm(h_ref[0], gain_ref[...]).astype(BF16)

    def proj(col0, width):
        return _dot(n, _bf16(w_in_ref, cols=slice(col0, col0 + width)))

    b_gate, c_gate, xc = proj(0, cw), proj(cw, cw), proj(2 * cw, cw)
    u = c_gate * xc
    u_ref[SUBLANES:SUBLANES + tile, :] = u

    cos = cos_ref[0]
    sin = sin_ref[0]
    lane = lax.broadcasted_iota(jnp.int32, (1, LANES), 1)
    first_half = (lane % HEAD_DIM) < HALF_DIM
    scale = HEAD_DIM ** -0.5 * LOG2_E
    for off, ref, mul in ((3 * cw, q_ref, scale), (3 * cw + aw, k_ref, 1.0)):
        xr = proj(off, aw)
        for g in range(aw // LANES):
            cols = slice(g * LANES, (g + 1) * LANES)
            xg = xr[:, cols]
            partner = jnp.where(first_half, pltpu.roll(xg, LANES - HALF_DIM, 1), pltpu.roll(xg, HALF_DIM, 1))
            ref[0, :, cols] = ((xg * cos + partner * sin) * mul).astype(ref.dtype)
    v_ref[0] = proj(3 * cw + 2 * aw, aw).astype(v_ref.dtype)
    base = 3 * cw + 3 * aw
    for c in range(d // PROJ_CHUNK):
        cols = slice(c * PROJ_CHUNK, (c + 1) * PROJ_CHUNK)
        ga_ref[0, :, cols] = _sigmoid(proj(base + d + c * PROJ_CHUNK, PROJ_CHUNK)).astype(ga_ref.dtype)
    conv_gate = [_sigmoid(proj(base + c * PROJ_CHUNK, PROJ_CHUNK)) for c in range(d // PROJ_CHUNK)]

    u_prev1 = u_ref[SUBLANES - 1:SUBLANES - 1 + tile, :]
    u_prev2 = u_ref[SUBLANES - 2:SUBLANES - 2 + tile, :]
    conv_w = conv_w_ref[...]
    conv = u_prev2 * conv_w[0:1, :] + u_prev1 * conv_w[1:2, :] + u * conv_w[2:3, :]
    u_ref[0:SUBLANES, :] = u[tile - SUBLANES:tile, :]
    gated = (b_gate * conv).astype(BF16)
    for c in range(d // PROJ_CHUNK):
        cols = slice(c * PROJ_CHUNK, (c + 1) * PROJ_CHUNK)
        y_conv = _dot(gated, _bf16(w_conv_out_ref, cols=cols))
        mc_ref[0, :, cols] = (conv_gate[c] * y_conv).astype(mc_ref.dtype)


def _mix_in(h, gains, w_in, conv_w, w_conv_out, cos, sin, layer):
    b, s, d = h.shape
    cw = conv_w.shape[2]
    aw = N_HEADS * HEAD_DIM
    tile = min(TOKEN_TILE, s)

    def rows(width):
        return pl.BlockSpec((1, tile, width), lambda i, j: (i, j, 0))

    return pl.pallas_call(
        functools.partial(_mix_in_kernel, conv_width=cw, attn_width=aw),
        grid=(b, s // tile),
        in_specs=[rows(d), _layer(gains.shape, layer), _layer(w_in.shape, layer), _layer(conv_w.shape, layer),
                  _layer(w_conv_out.shape, layer), rows(LANES), rows(LANES)],
        out_specs=[rows(aw), rows(aw), rows(aw), rows(d), rows(d)],
        out_shape=[jax.ShapeDtypeStruct((b, s, aw), BF16)] * 3 + [jax.ShapeDtypeStruct((b, s, d), BF16)] * 2,
        scratch_shapes=[pltpu.VMEM((tile + SUBLANES, cw), F32)],
        compiler_params=_params("parallel", "arbitrary"),
        name="mix_in",
    )(h, gains, w_in, conv_w, w_conv_out, cos, sin)


def _moba_kernel(q_ref, k_ref, v_ref, o_ref, kmean_ref, vt_ref, qt_ref, selb_ref, s0_ref, stat0_ref, s1_ref,
                 stat1_ref, m_ref, l_ref, acc_ref, *, n_blocks):
    qi = pl.program_id(1)
    blk = MOBA_BLOCK
    groups = q_ref.shape[2] // LANES

    @pl.when(qi == 0)
    def _():
        for n in range(n_blocks):
            rows = slice(n * blk, (n + 1) * blk)
            kmean_ref[n:n + 1, :] = jnp.mean(k_ref[0, rows, :].astype(F32), axis=0, keepdims=True)
            v_t = v_ref[0, rows, :].astype(F32).T.astype(BF16)
            for h in range(N_HEADS):
                vt_ref[n, h, 0:HEAD_DIM, :] = v_t[h * HEAD_DIM:(h + 1) * HEAD_DIM, :]
                vt_ref[n, h, HEAD_DIM:V_ROWS, :] = jnp.ones((V_ROWS - HEAD_DIM, blk), BF16)

    q_t = q_ref[0].astype(F32).T
    dim_row = lax.broadcasted_iota(jnp.int32, (LANES, 1), 0)
    blk_row = lax.broadcasted_iota(jnp.int32, (n_blocks, 1), 0)
    blk_row_f = blk_row.astype(F32)
    is_past = blk_row < qi
    for g in range(groups):
        kmean = kmean_ref[:, g * LANES:(g + 1) * LANES]
        kmean_hi = kmean.astype(BF16)
        kmean_lo = (kmean - kmean_hi.astype(F32)).astype(BF16)
        q_tg = q_t[g * LANES:(g + 1) * LANES, :]
        for hh in range(HEADS_PER_GROUP):
            h = g * HEADS_PER_GROUP + hh
            q_th = jnp.where((dim_row // HEAD_DIM) == hh, q_tg, 0.0).astype(BF16)
            qt_ref[h] = q_th
            gate = _dot(kmean_hi, q_th) + _dot(kmean_lo, q_th)
            gate = jnp.where(is_past, gate, -jnp.inf)
            keep = jnp.full((n_blocks, blk), MASKED, F32)
            for _ in range(TOP_K):
                best = jnp.max(gate, axis=0, keepdims=True)
                first = jnp.min(jnp.where(gate == best, blk_row_f, float(n_blocks)), axis=0, keepdims=True)
                pick = blk_row_f == first
                keep = jnp.where(pick, 0.0, keep)
                gate = jnp.where(pick, -jnp.inf, gate)
            selb_ref[h] = jnp.where(is_past, keep, MASKED)

    slots = ((s0_ref, stat0_ref), (s1_ref, stat1_ref))

    def score_stage(h, n, slot, diag):
        s_ref, stat_ref = slots[slot]
        g = h // HEADS_PER_GROUP
        start = pl.multiple_of(n * blk, blk)
        s_t = _dot(k_ref[0, pl.ds(start, blk), g * LANES:(g + 1) * LANES], qt_ref[h])
        if diag:
            key = lax.broadcasted_iota(jnp.int32, (blk, blk), 0)
            qry = lax.broadcasted_iota(jnp.int32, (blk, blk), 1)
            s_t = jnp.where(key <= qry, s_t, MASKED)
            keep = jnp.zeros((1, blk), F32)
        else:
            keep = selb_ref[h, pl.ds(n, 1), :]
        m_old = m_ref[h]
        m_new = jnp.maximum(m_old, jnp.max(s_t, axis=0, keepdims=True) + keep)
        s_ref[h] = s_t
        m_ref[h] = m_new
        stat_ref[h, 0:1, :] = m_new
        stat_ref[h, 1:2, :] = jnp.exp2(m_old - m_new)
        stat_ref[h, 2:3, :] = keep

    def value_stage(h, n, slot):
        s_ref, stat_ref = slots[slot]
        shift = stat_ref[h, 0:1, :]
        alpha = stat_ref[h, 1:2, :]
        keep = stat_ref[h, 2:3, :]
        p_t = jnp.exp2(s_ref[h] - shift).astype(BF16)
        pv_t = _dot(vt_ref[n, h], p_t)
        pv_t = jnp.where(keep == 0.0, pv_t, 0.0)
        acc_ref[h] = alpha * acc_ref[h] + pv_t[:HEAD_DIM]
        l_ref[h] = alpha * l_ref[h] + pv_t[HEAD_DIM:HEAD_DIM + 1]

    for h in range(N_HEADS):
        m_ref[h] = jnp.full((1, blk), MASKED, F32)
        l_ref[h] = jnp.zeros((1, blk), F32)
        acc_ref[h] = jnp.zeros((HEAD_DIM, blk), F32)
    for h in range(N_HEADS):
        score_stage(h, qi, 0, True)

    def step(n, slot):
        prev = jnp.where(n == 0, qi, n - 1)
        for h in range(N_HEADS):
            score_stage(h, n, 1 - slot, False)
            value_stage(h, prev, slot)

    def two_steps(t, carry):
        step(2 * t, 0)
        step(2 * t + 1, 1)
        return carry

    lax.fori_loop(0, qi // 2, two_steps, 0)

    @pl.when(qi % 2 == 1)
    def _():
        step(qi - 1, 0)
        for h in range(N_HEADS):
            value_stage(h, qi - 1, 1)

    @pl.when(qi % 2 == 0)
    def _():
        last = jnp.where(qi == 0, qi, qi - 1)
        for h in range(N_HEADS):
            value_stage(h, last, 0)

    for g in range(groups):
        o_t = jnp.concatenate(
            [acc_ref[g * HEADS_PER_GROUP + hh] / l_ref[g * HEADS_PER_GROUP + hh] for hh in range(HEADS_PER_GROUP)],
            axis=0)
        o_ref[0, :, g * LANES:(g + 1) * LANES] = o_t.T.astype(o_ref.dtype)


def _moba(q, k, v):
    b, s, aw = q.shape
    n_blocks = s // MOBA_BLOCK
    blk = MOBA_BLOCK
    qspec = pl.BlockSpec((1, blk, aw), lambda i, j: (i, j, 0))
    kvspec = pl.BlockSpec((1, s, aw), lambda i, j: (i, 0, 0))
    return pl.pallas_call(
        functools.partial(_moba_kernel, n_blocks=n_blocks),
        grid=(b, n_blocks),
        in_specs=[qspec, kvspec, kvspec],
        out_specs=qspec,
        out_shape=jax.ShapeDtypeStruct((b, s, aw), BF16),
        scratch_shapes=[
            pltpu.VMEM((n_blocks, aw), F32),
            pltpu.VMEM((n_blocks, N_HEADS, V_ROWS, blk), BF16),
            pltpu.VMEM((N_HEADS, LANES, blk), BF16),
            pltpu.VMEM((N_HEADS, n_blocks, blk), F32),
            pltpu.VMEM((N_HEADS, blk, blk), F32),
            pltpu.VMEM((N_HEADS, SUBLANES, blk), F32),
            pltpu.VMEM((N_HEADS, blk, blk), F32),
            pltpu.VMEM((N_HEADS, SUBLANES, blk), F32),
            pltpu.VMEM((N_HEADS, 1, blk), F32),
            pltpu.VMEM((N_HEADS, 1, blk), F32),
            pltpu.VMEM((N_HEADS, HEAD_DIM, blk), F32),
        ],
        compiler_params=_params("parallel", "arbitrary"),
        name="moba",
    )(q, k, v)


def kernel(x, p, positions, w_in, conv_w, w_conv_out, w_attn_out, w_o, ffn1_gate, ffn1_up, ffn1_down,
           ffn2_gate, ffn2_up, ffn2_down, norm_ffn1, norm_mix, norm_ffn2, norm_ple, w_ple_gate, w_ple_proj,
           norm_final):
    b, s, d = x.shape
    depth = w_in.shape[0]
    t = b * s
    assert s % MOBA_BLOCK == 0 and s % TOKEN_TILE == 0

    freq = jnp.arange(HALF_DIM, dtype=F32) / HALF_DIM
    inv_freq = jnp.tile(ROPE_THETA ** (-freq), LANES // HALF_DIM).reshape(1, LANES)
    cos, sin = _rope_tables(positions, inv_freq)

    def gains(g):
        return g.reshape(depth, 1, d)

    p_rows = p.reshape(depth, t, p.shape[-1])
    h = x.reshape(t, d)
    for i in range(depth):
        h = _ffn(h, gains(norm_ffn1), ffn1_gate, ffn1_up, ffn1_down, i)
        q, k, v, mc, ga = _mix_in(h.reshape(b, s, d), gains(norm_mix), w_in, conv_w, w_conv_out, cos, sin, i)
        o = _moba(q, k, v)
        h = _mix_ffn_ple(h, mc.reshape(t, d), ga.reshape(t, d), o.reshape(t, -1), p_rows, gains(norm_ffn2),
                         gains(norm_ple), norm_final.reshape(1, d), w_attn_out, w_o, ffn2_gate, ffn2_up,
                         ffn2_down, w_ple_gate, w_ple_proj, i, i == depth - 1)
    return h.reshape(b, s, d)
```

```python
import functools

import jax
import jax.numpy as jnp
from jax import lax
from jax.experimental import pallas as pl
from jax.experimental.pallas import tpu as pltpu

N_HEADS = 8
HEAD_DIM = 64
HALF_DIM = HEAD_DIM // 2
MOBA_BLOCK = 256
TOP_K = 3
ROPE_THETA = 10000.0
EPS = 1e-6

LANES = 128
SUBLANES = 8
HEADS_PER_GROUP = LANES // HEAD_DIM
BF16_SUBLANES = 2 * SUBLANES
V_ROWS = HEAD_DIM + BF16_SUBLANES
VMEM_LIMIT_BYTES = 56 * 1024 * 1024
MASKED = -1e30
LOG2_E = 1.4426950408889634

TOKEN_TILE = 512
STAGE_ROWS = 256
STAGE_DEPTH = 5
FF_CHUNK = 256
OUT_CHUNK = 256
PROJ_CHUNK = 512

F32 = jnp.float32
BF16 = jnp.bfloat16


def _dot(a, b):
    return jnp.dot(a, b, preferred_element_type=F32)


def _dot_nt(a, b):
    return lax.dot_general(a, b, (((1,), (1,)), ((), ())), preferred_element_type=F32)


def _rmsnorm(x, gain):
    ms = jnp.mean(x * x, axis=-1, keepdims=True)
    return x * lax.rsqrt(ms + EPS) * gain


def _sigmoid(x):
    return 1.0 / (1.0 + jnp.exp(-x))


def _resident(shape):
    nd = len(shape)
    return pl.BlockSpec(shape, lambda *_: (0,) * nd, pipeline_mode=pl.Buffered(1))


def _layer(stacked_shape, layer):
    nd = len(stacked_shape) - 1
    return pl.BlockSpec((None,) + tuple(stacked_shape[1:]), lambda *_: (layer,) + (0,) * nd,
                        pipeline_mode=pl.Buffered(1))


def _bf16(ref, rows=slice(None), cols=slice(None)):
    return ref[rows, cols].astype(BF16)


def _params(*semantics):
    return pltpu.CompilerParams(dimension_semantics=semantics, vmem_limit_bytes=VMEM_LIMIT_BYTES)


def _rope_table_kernel(pos_ref, inv_freq_ref, cos_ref, sin_ref):
    ang = pos_ref[0].astype(F32) * inv_freq_ref[...]
    lane = lax.broadcasted_iota(jnp.int32, (1, LANES), 1)
    first_half = (lane % HEAD_DIM) < HALF_DIM
    cos_ref[0] = jnp.cos(ang)
    sin = jnp.sin(ang)
    sin_ref[0] = jnp.where(first_half, -sin, sin)


def _rope_tables(positions, inv_freq):
    b, s = positions.shape
    tile = min(TOKEN_TILE, s)
    spec = pl.BlockSpec((1, tile, LANES), lambda i, j: (i, j, 0))
    return pl.pallas_call(
        _rope_table_kernel,
        grid=(b, s // tile),
        in_specs=[pl.BlockSpec((1, tile, 1), lambda i, j: (i, j, 0)), _resident((1, LANES))],
        out_specs=[spec, spec],
        out_shape=[jax.ShapeDtypeStruct((b, s, LANES), F32)] * 2,
        compiler_params=_params("parallel", "parallel"),
        name="rope_tables",
    )(positions.reshape(b, s, 1), inv_freq)


def _ffn_rows(h_ref, gain_ref, wg_ref, wu_ref, wd_ref, o_ref, act_ref):
    x = h_ref[...]
    n = _rmsnorm(x, gain_ref[...]).astype(BF16)
    d_ff = wg_ref.shape[1]
    for c in range(d_ff // FF_CHUNK):
        cols = slice(c * FF_CHUNK, (c + 1) * FF_CHUNK)
        g = _dot(n, _bf16(wg_ref, cols=cols))
        u = _dot(n, _bf16(wu_ref, cols=cols))
        act_ref[:, cols] = (g * _sigmoid(g) * u).astype(BF16)
    for c in range(x.shape[1] // OUT_CHUNK):
        cols = slice(c * OUT_CHUNK, (c + 1) * OUT_CHUNK)
        o_ref[:, cols] = x[:, cols] + 0.5 * _dot(act_ref[...], _bf16(wd_ref, cols=cols))


def _ffn(h, gains, wg, wu, wd, layer):
    t, d = h.shape
    d_ff = wg.shape[2]
    tile = min(TOKEN_TILE, t)
    row = pl.BlockSpec((tile, d), lambda i: (i, 0))
    return pl.pallas_call(
        _ffn_rows,
        grid=(t // tile,),
        in_specs=[row, _layer(gains.shape, layer), _layer(wg.shape, layer), _layer(wu.shape, layer),
                  _layer(wd.shape, layer)],
        out_specs=row,
        out_shape=jax.ShapeDtypeStruct((t, d), F32),
        scratch_shapes=[pltpu.VMEM((tile, d_ff), BF16)],
        compiler_params=_params("parallel"),
        name="ffn",
    )(h, gains, wg, wu, wd)


def _col_windows(src_hbm, layer, dst_ref, width):
    return [(src_hbm.at[layer, :, c:c + width], dst_ref.at[:, c:c + width])
            for c in range(0, dst_ref.shape[1], width)]


def _row_windows(src_hbm, layer, dst_ref, height):
    return [(src_hbm.at[layer, r:r + height, :], dst_ref.at[r:r + height, :])
            for r in range(0, dst_ref.shape[0], height)]


def _stage_params(jobs, ring_ref, sem_ref):
    depth = ring_ref.shape[0]

    def copy(i):
        return pltpu.make_async_copy(jobs[i][0], ring_ref.at[i % depth], sem_ref.at[i % depth])

    for i in range(min(depth - 1, len(jobs))):
        copy(i).start()
    for i, (_, dst) in enumerate(jobs):
        if i + depth - 1 < len(jobs):
            copy(i + depth - 1).start()
        copy(i).wait()
        dst[...] = ring_ref[i % depth].astype(BF16)


def _mix_ffn_ple_kernel(h_ref, mc_ref, ga_ref, attn_ref, p_ref, gain_ref, ple_gain_ref, final_gain_ref,
                        w_attn_out_hbm, w_o_hbm, wg_hbm, wu_hbm, wd_hbm, w_gate_hbm, w_proj_hbm, o_ref,
                        w_attn_out_ref, w_o_ref, wg_ref, wu_ref, wd_ref, w_gate_ref, w_proj_ref,
                        col_ring, row_ring, col_sem, row_sem, act_ref, x_ref, *, layer, final):
    @pl.when(pl.program_id(0) == 0)
    def _():
        row_jobs = (_row_windows(w_attn_out_hbm, layer, w_attn_out_ref, STAGE_ROWS)
                    + _row_windows(wd_hbm, layer, wd_ref, STAGE_ROWS)
                    + _row_windows(w_proj_hbm, layer, w_proj_ref, STAGE_ROWS))
        col_jobs = _col_windows(w_o_hbm, layer, w_o_ref, FF_CHUNK)
        for gate_job, up_job in zip(_col_windows(wg_hbm, layer, wg_ref, FF_CHUNK),
                                    _col_windows(wu_hbm, layer, wu_ref, FF_CHUNK)):
            col_jobs += [gate_job, up_job]
        col_jobs += _col_windows(w_gate_hbm, layer, w_gate_ref, FF_CHUNK)
        _stage_params(row_jobs, row_ring, row_sem)
        _stage_params(col_jobs, col_ring, col_sem)

    y_attn = _dot(attn_ref[...], w_attn_out_ref[...])
    merged = mc_ref[...].astype(F32) + ga_ref[...].astype(F32) * y_attn
    x_ref[...] = h_ref[...] + _dot(merged.astype(BF16), w_o_ref[...])
    _ffn_rows(x_ref, gain_ref, wg_ref, wu_ref, wd_ref, o_ref, act_ref)
    y = o_ref[...]
    n = _rmsnorm(y, ple_gain_ref[...]).astype(BF16)
    gate = _sigmoid(_dot(n, w_gate_ref[...]))
    y = y + gate * _dot(p_ref[...].astype(BF16), w_proj_ref[...])
    o_ref[...] = _rmsnorm(y, final_gain_ref[...]) if final else y


def _mix_ffn_ple(h, mc, ga, attn, p, gains, ple_gains, final_gain, w_attn_out, w_o, wg, wu, wd, w_gate, w_proj,
                 layer, final):
    t, d = h.shape
    d_ff = wg.shape[2]
    aw = attn.shape[1]
    pd = p.shape[2]
    tile = min(TOKEN_TILE, t)

    def rows(width):
        return pl.BlockSpec((tile, width), lambda i: (i, 0))

    hbm = pl.BlockSpec(memory_space=pl.ANY)
    return pl.pallas_call(
        functools.partial(_mix_ffn_ple_kernel, layer=layer, final=final),
        grid=(t // tile,),
        in_specs=[rows(d), rows(d), rows(d), rows(aw), pl.BlockSpec((None, tile, pd), lambda i: (layer, i, 0)),
                  _layer(gains.shape, layer), _layer(ple_gains.shape, layer), _resident(final_gain.shape)]
                 + [hbm] * 7,
        out_specs=rows(d),
        out_shape=jax.ShapeDtypeStruct((t, d), F32),
        scratch_shapes=[
            pltpu.VMEM((aw, d), BF16), pltpu.VMEM((d, d), BF16), pltpu.VMEM((d, d_ff), BF16),
            pltpu.VMEM((d, d_ff), BF16), pltpu.VMEM((d_ff, d), BF16), pltpu.VMEM((d, d), BF16),
            pltpu.VMEM((pd, d), BF16),
            pltpu.VMEM((STAGE_DEPTH, d, FF_CHUNK), F32), pltpu.VMEM((STAGE_DEPTH, STAGE_ROWS, d), F32),
            pltpu.SemaphoreType.DMA((STAGE_DEPTH,)), pltpu.SemaphoreType.DMA((STAGE_DEPTH,)),
            pltpu.VMEM((tile, d_ff), BF16), pltpu.VMEM((tile, d), F32),
        ],
        compiler_params=_params("arbitrary"),
        name="mix_ffn_ple",
    )(h, mc, ga, attn, p, gains, ple_gains, final_gain, w_attn_out, w_o, wg, wu, wd, w_gate, w_proj)


def _mix_in_kernel(h_ref, gain_ref, w_in_ref, conv_w_ref, w_conv_out_ref, cos_ref, sin_ref,
                   qt_ref, k_ref, vt_ref, mc_ref, ga_ref, u_ref, *, conv_width, attn_width):
    tile = h_ref.shape[1]
    cw, aw = conv_width, attn_width
    d = h_ref.shape[2]
    @pl.when(pl.program_id(1) == 0)
    def _():
        u_ref[0:SUBLANES, :] = jnp.zeros((SUBLANES, cw), F32)

    n = _rmsnorm(h_ref[0], gain_ref[...]).astype(BF16)

    def proj(col0, width):
        return _dot(n, _bf16(w_in_ref, cols=slice(col0, col0 + width)))

    b_gate, c_gate, xc = proj(0, cw), proj(cw, cw), proj(2 * cw, cw)
    u = c_gate * xc
    u_ref[SUBLANES:SUBLANES + tile, :] = u

    cos = cos_ref[0]
    sin = sin_ref[0]
    lane = lax.broadcasted_iota(jnp.int32, (1, LANES), 1)
    first_half = (lane % HEAD_DIM) < HALF_DIM
    scale = HEAD_DIM ** -0.5 * LOG2_E
    dim_row = lax.broadcasted_iota(jnp.int32, (LANES, 1), 0)
    blocks = range(tile // MOBA_BLOCK)

    def rotary(xg):
        partner = jnp.where(first_half, pltpu.roll(xg, LANES - HALF_DIM, 1), pltpu.roll(xg, HALF_DIM, 1))
        return xg * cos + partner * sin

    q_rows = proj(3 * cw, aw)
    for g in range(aw // LANES):
        q_t = (rotary(q_rows[:, g * LANES:(g + 1) * LANES]) * scale).T
        for hh in range(HEADS_PER_GROUP):
            q_th = jnp.where((dim_row // HEAD_DIM) == hh, q_t, 0.0).astype(qt_ref.dtype)
            for j in blocks:
                qt_ref[0, j, g * HEADS_PER_GROUP + hh] = q_th[:, j * MOBA_BLOCK:(j + 1) * MOBA_BLOCK]
    k_rows = proj(3 * cw + aw, aw)
    for g in range(aw // LANES):
        cols = slice(g * LANES, (g + 1) * LANES)
        k_ref[0, :, cols] = rotary(k_rows[:, cols]).astype(k_ref.dtype)
    v_rows = proj(3 * cw + 2 * aw, aw)
    ones = jnp.ones((V_ROWS - HEAD_DIM, MOBA_BLOCK), vt_ref.dtype)
    for g in range(aw // LANES):
        v_t = v_rows[:, g * LANES:(g + 1) * LANES].T.astype(vt_ref.dtype)
        for hh in range(HEADS_PER_GROUP):
            for j in blocks:
                head = g * HEADS_PER_GROUP + hh
                vt_ref[0, j, head, 0:HEAD_DIM, :] = v_t[hh * HEAD_DIM:(hh + 1) * HEAD_DIM,
                                                        j * MOBA_BLOCK:(j + 1) * MOBA_BLOCK]
                vt_ref[0, j, head, HEAD_DIM:V_ROWS, :] = ones
    base = 3 * cw + 3 * aw
    for c in range(d // PROJ_CHUNK):
        cols = slice(c * PROJ_CHUNK, (c + 1) * PROJ_CHUNK)
        ga_ref[0, :, cols] = _sigmoid(proj(base + d + c * PROJ_CHUNK, PROJ_CHUNK)).astype(ga_ref.dtype)
    conv_gate = [_sigmoid(proj(base + c * PROJ_CHUNK, PROJ_CHUNK)) for c in range(d // PROJ_CHUNK)]

    u_prev1 = u_ref[SUBLANES - 1:SUBLANES - 1 + tile, :]
    u_prev2 = u_ref[SUBLANES - 2:SUBLANES - 2 + tile, :]
    conv_w = conv_w_ref[...]
    conv = u_prev2 * conv_w[0:1, :] + u_prev1 * conv_w[1:2, :] + u * conv_w[2:3, :]
    u_ref[0:SUBLANES, :] = u[tile - SUBLANES:tile, :]
    gated = (b_gate * conv).astype(BF16)
    for c in range(d // PROJ_CHUNK):
        cols = slice(c * PROJ_CHUNK, (c + 1) * PROJ_CHUNK)
        y_conv = _dot(gated, _bf16(w_conv_out_ref, cols=cols))
        mc_ref[0, :, cols] = (conv_gate[c] * y_conv).astype(mc_ref.dtype)


def _mix_in(h, gains, w_in, conv_w, w_conv_out, cos, sin, layer):
    b, s, d = h.shape
    cw = conv_w.shape[2]
    aw = N_HEADS * HEAD_DIM
    tile = min(TOKEN_TILE, s)

    n_blocks = s // MOBA_BLOCK

    def rows(width):
        return pl.BlockSpec((1, tile, width), lambda i, j: (i, j, 0))

    def per_block(height):
        return pl.BlockSpec((1, tile // MOBA_BLOCK, N_HEADS, height, MOBA_BLOCK), lambda i, j: (i, j, 0, 0, 0))

    return pl.pallas_call(
        functools.partial(_mix_in_kernel, conv_width=cw, attn_width=aw),
        grid=(b, s // tile),
        in_specs=[rows(d), _layer(gains.shape, layer), _layer(w_in.shape, layer), _layer(conv_w.shape, layer),
                  _layer(w_conv_out.shape, layer), rows(LANES), rows(LANES)],
        out_specs=[per_block(LANES), rows(aw), per_block(V_ROWS), rows(d), rows(d)],
        out_shape=[jax.ShapeDtypeStruct((b, n_blocks, N_HEADS, LANES, MOBA_BLOCK), BF16),
                   jax.ShapeDtypeStruct((b, s, aw), BF16),
                   jax.ShapeDtypeStruct((b, n_blocks, N_HEADS, V_ROWS, MOBA_BLOCK), BF16),
                   jax.ShapeDtypeStruct((b, s, d), BF16), jax.ShapeDtypeStruct((b, s, d), BF16)],
        scratch_shapes=[pltpu.VMEM((tile + SUBLANES, cw), F32)],
        compiler_params=_params("parallel", "arbitrary"),
        name="mix_in",
    )(h, gains, w_in, conv_w, w_conv_out, cos, sin)


def _moba_kernel(qt_ref, k_ref, vt_ref, o_ref, kmean_ref, selb_ref, s0_ref, stat0_ref, s1_ref, stat1_ref, m_ref,
                 l_ref, acc_ref, *, n_blocks):
    qi = pl.program_id(1)
    blk = MOBA_BLOCK
    groups = N_HEADS // HEADS_PER_GROUP

    @pl.when(qi == 0)
    def _():
        for n in range(n_blocks):
            rows = slice(n * blk, (n + 1) * blk)
            kmean_ref[n:n + 1, :] = jnp.mean(k_ref[0, rows, :].astype(F32), axis=0, keepdims=True)

    blk_row = lax.broadcasted_iota(jnp.int32, (n_blocks, 1), 0)
    blk_row_f = blk_row.astype(F32)
    is_past = blk_row < qi
    for g in range(groups):
        kmean = kmean_ref[:, g * LANES:(g + 1) * LANES]
        kmean_hi = kmean.astype(BF16)
        kmean_lo = (kmean - kmean_hi.astype(F32)).astype(BF16)
        for hh in range(HEADS_PER_GROUP):
            h = g * HEADS_PER_GROUP + hh
            q_th = qt_ref[0, 0, h]
            gate = _dot(kmean_hi, q_th) + _dot(kmean_lo, q_th)
            gate = jnp.where(is_past, gate, -jnp.inf)
            keep = jnp.full((n_blocks, blk), MASKED, F32)
            for _ in range(TOP_K):
                best = jnp.max(gate, axis=0, keepdims=True)
                first = jnp.min(jnp.where(gate == best, blk_row_f, float(n_blocks)), axis=0, keepdims=True)
                pick = blk_row_f == first
                keep = jnp.where(pick, 0.0, keep)
                gate = jnp.where(pick, -jnp.inf, gate)
            selb_ref[h] = jnp.where(is_past, keep, MASKED)

    slots = ((s0_ref, stat0_ref), (s1_ref, stat1_ref))

    def score_stage(h, n, slot, diag):
        s_ref, stat_ref = slots[slot]
        g = h // HEADS_PER_GROUP
        start = pl.multiple_of(n * blk, blk)
        s_t = _dot(k_ref[0, pl.ds(start, blk), g * LANES:(g + 1) * LANES], qt_ref[0, 0, h])
        if diag:
            key = lax.broadcasted_iota(jnp.int32, (blk, blk), 0)
            qry = lax.broadcasted_iota(jnp.int32, (blk, blk), 1)
            s_t = jnp.where(key <= qry, s_t, MASKED)
            keep = jnp.zeros((1, blk), F32)
        else:
            keep = selb_ref[h, pl.ds(n, 1), :]
        m_old = m_ref[h]
        m_new = jnp.maximum(m_old, jnp.max(s_t, axis=0, keepdims=True) + keep)
        s_ref[h] = s_t
        m_ref[h] = m_new
        stat_ref[h, 0:1, :] = m_new
        stat_ref[h, 1:2, :] = jnp.exp2(m_old - m_new)
        stat_ref[h, 2:3, :] = keep

    def value_stage(h, n, slot):
        s_ref, stat_ref = slots[slot]
        shift = stat_ref[h, 0:1, :]
        alpha = stat_ref[h, 1:2, :]
        keep = stat_ref[h, 2:3, :]
        p_t = jnp.exp2(s_ref[h] - shift).astype(BF16)
        pv_t = _dot(vt_ref[0, n, h], p_t)
        pv_t = jnp.where(keep == 0.0, pv_t, 0.0)
        acc_ref[h] = alpha * acc_ref[h] + pv_t[:HEAD_DIM]
        l_ref[h] = alpha * l_ref[h] + pv_t[HEAD_DIM:HEAD_DIM + 1]

    for h in range(N_HEADS):
        m_ref[h] = jnp.full((1, blk), MASKED, F32)
        l_ref[h] = jnp.zeros((1, blk), F32)
        acc_ref[h] = jnp.zeros((HEAD_DIM, blk), F32)
    for h in range(N_HEADS):
        score_stage(h, qi, 0, True)

    def step(n, slot):
        prev = jnp.where(n == 0, qi, n - 1)
        for h in range(N_HEADS):
            score_stage(h, n, 1 - slot, False)
            value_stage(h, prev, slot)

    def two_steps(t, carry):
        step(2 * t, 0)
        step(2 * t + 1, 1)
        return carry

    lax.fori_loop(0, qi // 2, two_steps, 0)

    @pl.when(qi % 2 == 1)
    def _():
        step(qi - 1, 0)
        for h in range(N_HEADS):
            value_stage(h, qi - 1, 1)

    @pl.when(qi % 2 == 0)
    def _():
        last = jnp.where(qi == 0, qi, qi - 1)
        for h in range(N_HEADS):
            value_stage(h, last, 0)

    for g in range(groups):
        o_t = jnp.concatenate(
            [acc_ref[g * HEADS_PER_GROUP + hh] / l_ref[g * HEADS_PER_GROUP + hh] for hh in range(HEADS_PER_GROUP)],
            axis=0)
        o_ref[0, :, g * LANES:(g + 1) * LANES] = o_t.T.astype(o_ref.dtype)


def _moba(qt, k, vt):
    b, s, aw = k.shape
    n_blocks = s // MOBA_BLOCK
    blk = MOBA_BLOCK
    return pl.pallas_call(
        functools.partial(_moba_kernel, n_blocks=n_blocks),
        grid=(b, n_blocks),
        in_specs=[pl.BlockSpec((1, 1) + qt.shape[2:], lambda i, j: (i, j, 0, 0, 0)),
                  pl.BlockSpec((1, s, aw), lambda i, j: (i, 0, 0)),
                  pl.BlockSpec((1,) + vt.shape[1:], lambda i, j: (i, 0, 0, 0, 0))],
        out_specs=pl.BlockSpec((1, blk, aw), lambda i, j: (i, j, 0)),
        out_shape=jax.ShapeDtypeStruct((b, s, aw), BF16),
        scratch_shapes=[
            pltpu.VMEM((n_blocks, aw), F32),
            pltpu.VMEM((N_HEADS, n_blocks, blk), F32),
            pltpu.VMEM((N_HEADS, blk, blk), F32),
            pltpu.VMEM((N_HEADS, SUBLANES, blk), F32),
            pltpu.VMEM((N_HEADS, blk, blk), F32),
            pltpu.VMEM((N_HEADS, SUBLANES, blk), F32),
            pltpu.VMEM((N_HEADS, 1, blk), F32),
            pltpu.VMEM((N_HEADS, 1, blk), F32),
            pltpu.VMEM((N_HEADS, HEAD_DIM, blk), F32),
        ],
        compiler_params=_params("parallel", "arbitrary"),
        name="moba",
    )(qt, k, vt)


def kernel(x, p, positions, w_in, conv_w, w_conv_out, w_attn_out, w_o, ffn1_gate, ffn1_up, ffn1_down,
           ffn2_gate, ffn2_up, ffn2_down, norm_ffn1, norm_mix, norm_ffn2, norm_ple, w_ple_gate, w_ple_proj,
           norm_final):
    b, s, d = x.shape
    depth = w_in.shape[0]
    t = b * s
    assert s % MOBA_BLOCK == 0 and s % TOKEN_TILE == 0

    freq = jnp.arange(HALF_DIM, dtype=F32) / HALF_DIM
    inv_freq = jnp.tile(ROPE_THETA ** (-freq), LANES // HALF_DIM).reshape(1, LANES)
    cos, sin = _rope_tables(positions, inv_freq)

    def gains(g):
        return g.reshape(depth, 1, d)

    p_rows = p.reshape(depth, t, p.shape[-1])
    h = x.reshape(t, d)
    for i in range(depth):
        h = _ffn(h, gains(norm_ffn1), ffn1_gate, ffn1_up, ffn1_down, i)
        qt, k, vt, mc, ga = _mix_in(h.reshape(b, s, d), gains(norm_mix), w_in, conv_w, w_conv_out, cos, sin, i)
        o = _moba(qt, k, vt)
        h = _mix_ffn_ple(h, mc.reshape(t, d), ga.reshape(t, d), o.reshape(t, -1), p_rows, gains(norm_ffn2),
                         gains(norm_ple), norm_final.reshape(1, d), w_attn_out, w_o, ffn2_gate, ffn2_up,
                         ffn2_down, w_ple_gate, w_ple_proj, i, i == depth - 1)
    return h.reshape(b, s, d)
```

```python
import functools

import jax
import jax.numpy as jnp
from jax import lax
from jax.experimental import pallas as pl
from jax.experimental.pallas import tpu as pltpu

N_HEADS = 8
HEAD_DIM = 64
HALF_DIM = HEAD_DIM // 2
MOBA_BLOCK = 256
TOP_K = 3
ROPE_THETA = 10000.0
EPS = 1e-6

LANES = 128
SUBLANES = 8
HEADS_PER_GROUP = LANES // HEAD_DIM
BF16_SUBLANES = 2 * SUBLANES
V_ROWS = HEAD_DIM + BF16_SUBLANES
VMEM_LIMIT_BYTES = 56 * 1024 * 1024
MASKED = -1e30
LOG2_E = 1.4426950408889634

TOKEN_TILE = 512
STAGE_ROWS = 256
STAGE_DEPTH = 5
FF_CHUNK = 256
OUT_CHUNK = 256
PROJ_CHUNK = 512

F32 = jnp.float32
BF16 = jnp.bfloat16


def _dot(a, b):
    return jnp.dot(a, b, preferred_element_type=F32)


def _dot_nt(a, b):
    return lax.dot_general(a, b, (((1,), (1,)), ((), ())), preferred_element_type=F32)


def _rmsnorm(x, gain):
    ms = jnp.mean(x * x, axis=-1, keepdims=True)
    return x * lax.rsqrt(ms + EPS) * gain


def _sigmoid(x):
    return 1.0 / (1.0 + jnp.exp(-x))


def _resident(shape):
    nd = len(shape)
    return pl.BlockSpec(shape, lambda *_: (0,) * nd, pipeline_mode=pl.Buffered(1))


def _layer(stacked_shape, layer):
    nd = len(stacked_shape) - 1
    return pl.BlockSpec((None,) + tuple(stacked_shape[1:]), lambda *_: (layer,) + (0,) * nd,
                        pipeline_mode=pl.Buffered(1))


def _bf16(ref, rows=slice(None), cols=slice(None)):
    return ref[rows, cols].astype(BF16)


def _params(*semantics):
    return pltpu.CompilerParams(dimension_semantics=semantics, vmem_limit_bytes=VMEM_LIMIT_BYTES)


def _rope_table_kernel(pos_ref, inv_freq_ref, cos_ref, sin_ref):
    ang = pos_ref[0].astype(F32) * inv_freq_ref[...]
    lane = lax.broadcasted_iota(jnp.int32, (1, LANES), 1)
    first_half = (lane % HEAD_DIM) < HALF_DIM
    cos_ref[0] = jnp.cos(ang)
    sin = jnp.sin(ang)
    sin_ref[0] = jnp.where(first_half, -sin, sin)


def _rope_tables(positions, inv_freq):
    b, s = positions.shape
    tile = min(TOKEN_TILE, s)
    spec = pl.BlockSpec((1, tile, LANES), lambda i, j: (i, j, 0))
    return pl.pallas_call(
        _rope_table_kernel,
        grid=(b, s // tile),
        in_specs=[pl.BlockSpec((1, tile, 1), lambda i, j: (i, j, 0)), _resident((1, LANES))],
        out_specs=[spec, spec],
        out_shape=[jax.ShapeDtypeStruct((b, s, LANES), F32)] * 2,
        compiler_params=_params("parallel", "parallel"),
        name="rope_tables",
    )(positions.reshape(b, s, 1), inv_freq)


def _ffn_rows(h_ref, gain_ref, wg_ref, wu_ref, wd_ref, o_ref, act_ref):
    x = h_ref[...]
    n = _rmsnorm(x, gain_ref[...]).astype(BF16)
    d_ff = wg_ref.shape[1]
    for c in range(d_ff // FF_CHUNK):
        cols = slice(c * FF_CHUNK, (c + 1) * FF_CHUNK)
        g = _dot(n, _bf16(wg_ref, cols=cols))
        u = _dot(n, _bf16(wu_ref, cols=cols))
        act_ref[:, cols] = (g * _sigmoid(g) * u).astype(BF16)
    for c in range(x.shape[1] // OUT_CHUNK):
        cols = slice(c * OUT_CHUNK, (c + 1) * OUT_CHUNK)
        o_ref[:, cols] = x[:, cols] + 0.5 * _dot(act_ref[...], _bf16(wd_ref, cols=cols))


def _ffn(h, gains, wg, wu, wd, layer):
    t, d = h.shape
    d_ff = wg.shape[2]
    tile = min(TOKEN_TILE, t)
    row = pl.BlockSpec((tile, d), lambda i: (i, 0))
    return pl.pallas_call(
        _ffn_rows,
        grid=(t // tile,),
        in_specs=[row, _layer(gains.shape, layer), _layer(wg.shape, layer), _layer(wu.shape, layer),
                  _layer(wd.shape, layer)],
        out_specs=row,
        out_shape=jax.ShapeDtypeStruct((t, d), F32),
        scratch_shapes=[pltpu.VMEM((tile, d_ff), BF16)],
        compiler_params=_params("parallel"),
        name="ffn",
    )(h, gains, wg, wu, wd)


def _col_windows(src_hbm, layer, dst_ref, width):
    return [(src_hbm.at[layer, :, c:c + width], dst_ref.at[:, c:c + width])
            for c in range(0, dst_ref.shape[1], width)]


def _row_windows(src_hbm, layer, dst_ref, height):
    return [(src_hbm.at[layer, r:r + height, :], dst_ref.at[r:r + height, :])
            for r in range(0, dst_ref.shape[0], height)]


def _stage_params(jobs, ring_ref, sem_ref):
    depth = ring_ref.shape[0]

    def copy(i):
        return pltpu.make_async_copy(jobs[i][0], ring_ref.at[i % depth], sem_ref.at[i % depth])

    for i in range(min(depth - 1, len(jobs))):
        copy(i).start()
    for i, (_, dst) in enumerate(jobs):
        if i + depth - 1 < len(jobs):
            copy(i + depth - 1).start()
        copy(i).wait()
        dst[...] = ring_ref[i % depth].astype(BF16)


def _mix_ffn_ple_kernel(h_ref, mc_ref, ga_ref, attn_ref, p_ref, gain_ref, ple_gain_ref, final_gain_ref,
                        w_attn_out_hbm, w_o_hbm, wg_hbm, wu_hbm, wd_hbm, w_gate_hbm, w_proj_hbm, o_ref,
                        w_attn_out_ref, w_o_ref, wg_ref, wu_ref, wd_ref, w_gate_ref, w_proj_ref,
                        col_ring, row_ring, col_sem, row_sem, act_ref, x_ref, *, layer, final):
    @pl.when(pl.program_id(0) == 0)
    def _():
        row_jobs = (_row_windows(w_attn_out_hbm, layer, w_attn_out_ref, STAGE_ROWS)
                    + _row_windows(wd_hbm, layer, wd_ref, STAGE_ROWS)
                    + _row_windows(w_proj_hbm, layer, w_proj_ref, STAGE_ROWS))
        col_jobs = _col_windows(w_o_hbm, layer, w_o_ref, FF_CHUNK)
        for gate_job, up_job in zip(_col_windows(wg_hbm, layer, wg_ref, FF_CHUNK),
                                    _col_windows(wu_hbm, layer, wu_ref, FF_CHUNK)):
            col_jobs += [gate_job, up_job]
        col_jobs += _col_windows(w_gate_hbm, layer, w_gate_ref, FF_CHUNK)
        _stage_params(row_jobs, row_ring, row_sem)
        _stage_params(col_jobs, col_ring, col_sem)

    y_attn = _dot(attn_ref[...], w_attn_out_ref[...])
    merged = mc_ref[...].astype(F32) + ga_ref[...].astype(F32) * y_attn
    x_ref[...] = h_ref[...] + _dot(merged.astype(BF16), w_o_ref[...])
    _ffn_rows(x_ref, gain_ref, wg_ref, wu_ref, wd_ref, o_ref, act_ref)
    y = o_ref[...]
    n = _rmsnorm(y, ple_gain_ref[...]).astype(BF16)
    gate = _sigmoid(_dot(n, w_gate_ref[...]))
    y = y + gate * _dot(p_ref[...].astype(BF16), w_proj_ref[...])
    o_ref[...] = _rmsnorm(y, final_gain_ref[...]) if final else y


def _mix_ffn_ple(h, mc, ga, attn, p, gains, ple_gains, final_gain, w_attn_out, w_o, wg, wu, wd, w_gate, w_proj,
                 layer, final):
    t, d = h.shape
    d_ff = wg.shape[2]
    aw = attn.shape[1]
    pd = p.shape[2]
    tile = min(TOKEN_TILE, t)

    def rows(width):
        return pl.BlockSpec((tile, width), lambda i: (i, 0))

    hbm = pl.BlockSpec(memory_space=pl.ANY)
    return pl.pallas_call(
        functools.partial(_mix_ffn_ple_kernel, layer=layer, final=final),
        grid=(t // tile,),
        in_specs=[rows(d), rows(d), rows(d), rows(aw), pl.BlockSpec((None, tile, pd), lambda i: (layer, i, 0)),
                  _layer(gains.shape, layer), _layer(ple_gains.shape, layer), _resident(final_gain.shape)]
                 + [hbm] * 7,
        out_specs=rows(d),
        out_shape=jax.ShapeDtypeStruct((t, d), F32),
        scratch_shapes=[
            pltpu.VMEM((aw, d), BF16), pltpu.VMEM((d, d), BF16), pltpu.VMEM((d, d_ff), BF16),
            pltpu.VMEM((d, d_ff), BF16), pltpu.VMEM((d_ff, d), BF16), pltpu.VMEM((d, d), BF16),
            pltpu.VMEM((pd, d), BF16),
            pltpu.VMEM((STAGE_DEPTH, d, FF_CHUNK), F32), pltpu.VMEM((STAGE_DEPTH, STAGE_ROWS, d), F32),
            pltpu.SemaphoreType.DMA((STAGE_DEPTH,)), pltpu.SemaphoreType.DMA((STAGE_DEPTH,)),
            pltpu.VMEM((tile, d_ff), BF16), pltpu.VMEM((tile, d), F32),
        ],
        compiler_params=_params("arbitrary"),
        name="mix_ffn_ple",
    )(h, mc, ga, attn, p, gains, ple_gains, final_gain, w_attn_out, w_o, wg, wu, wd, w_gate, w_proj)


def _mix_in_kernel(h_ref, gain_ref, w_in_ref, conv_w_ref, w_conv_out_ref, cos_ref, sin_ref,
                   qt_ref, k_ref, vt_ref, mc_ref, ga_ref, u_ref, *, conv_width, attn_width):
    tile = h_ref.shape[1]
    cw, aw = conv_width, attn_width
    d = h_ref.shape[2]
    @pl.when(pl.program_id(1) == 0)
    def _():
        u_ref[0:SUBLANES, :] = jnp.zeros((SUBLANES, cw), F32)

    n = _rmsnorm(h_ref[0], gain_ref[...]).astype(BF16)

    def proj(col0, width):
        return _dot(n, _bf16(w_in_ref, cols=slice(col0, col0 + width)))

    b_gate, c_gate, xc = proj(0, cw), proj(cw, cw), proj(2 * cw, cw)
    u = c_gate * xc
    u_ref[SUBLANES:SUBLANES + tile, :] = u

    cos = cos_ref[0]
    sin = sin_ref[0]
    lane = lax.broadcasted_iota(jnp.int32, (1, LANES), 1)
    first_half = (lane % HEAD_DIM) < HALF_DIM
    scale = HEAD_DIM ** -0.5 * LOG2_E
    dim_row = lax.broadcasted_iota(jnp.int32, (LANES, 1), 0)
    blocks = range(tile // MOBA_BLOCK)

    def rotary(xg):
        partner = jnp.where(first_half, pltpu.roll(xg, LANES - HALF_DIM, 1), pltpu.roll(xg, HALF_DIM, 1))
        return xg * cos + partner * sin

    q_rows = proj(3 * cw, aw)
    for g in range(aw // LANES):
        q_t = (rotary(q_rows[:, g * LANES:(g + 1) * LANES]) * scale).T
        for hh in range(HEADS_PER_GROUP):
            q_th = jnp.where((dim_row // HEAD_DIM) == hh, q_t, 0.0).astype(qt_ref.dtype)
            for j in blocks:
                qt_ref[0, j, g * HEADS_PER_GROUP + hh] = q_th[:, j * MOBA_BLOCK:(j + 1) * MOBA_BLOCK]
    k_rows = proj(3 * cw + aw, aw)
    for g in range(aw // LANES):
        cols = slice(g * LANES, (g + 1) * LANES)
        k_ref[0, :, cols] = rotary(k_rows[:, cols]).astype(k_ref.dtype)
    v_rows = proj(3 * cw + 2 * aw, aw)
    ones = jnp.ones((V_ROWS - HEAD_DIM, MOBA_BLOCK), vt_ref.dtype)
    for g in range(aw // LANES):
        v_t = v_rows[:, g * LANES:(g + 1) * LANES].T.astype(vt_ref.dtype)
        for hh in range(HEADS_PER_GROUP):
            for j in blocks:
                head = g * HEADS_PER_GROUP + hh
                vt_ref[0, j, head, 0:HEAD_DIM, :] = v_t[hh * HEAD_DIM:(hh + 1) * HEAD_DIM,
                                                        j * MOBA_BLOCK:(j + 1) * MOBA_BLOCK]
                vt_ref[0, j, head, HEAD_DIM:V_ROWS, :] = ones
    base = 3 * cw + 3 * aw
    for c in range(d // PROJ_CHUNK):
        cols = slice(c * PROJ_CHUNK, (c + 1) * PROJ_CHUNK)
        ga_ref[0, :, cols] = _sigmoid(proj(base + d + c * PROJ_CHUNK, PROJ_CHUNK)).astype(ga_ref.dtype)
    conv_gate = [_sigmoid(proj(base + c * PROJ_CHUNK, PROJ_CHUNK)) for c in range(d // PROJ_CHUNK)]

    u_prev1 = u_ref[SUBLANES - 1:SUBLANES - 1 + tile, :]
    u_prev2 = u_ref[SUBLANES - 2:SUBLANES - 2 + tile, :]
    conv_w = conv_w_ref[...]
    conv = u_prev2 * conv_w[0:1, :] + u_prev1 * conv_w[1:2, :] + u * conv_w[2:3, :]
    u_ref[0:SUBLANES, :] = u[tile - SUBLANES:tile, :]
    gated = (b_gate * conv).astype(BF16)
    for c in range(d // PROJ_CHUNK):
        cols = slice(c * PROJ_CHUNK, (c + 1) * PROJ_CHUNK)
        y_conv = _dot(gated, _bf16(w_conv_out_ref, cols=cols))
        mc_ref[0, :, cols] = (conv_gate[c] * y_conv).astype(mc_ref.dtype)


def _mix_in(h, gains, w_in, conv_w, w_conv_out, cos, sin, layer):
    b, s, d = h.shape
    cw = conv_w.shape[2]
    aw = N_HEADS * HEAD_DIM
    tile = min(TOKEN_TILE, s)

    n_blocks = s // MOBA_BLOCK

    def rows(width):
        return pl.BlockSpec((1, tile, width), lambda i, j: (i, j, 0))

    def per_block(height):
        return pl.BlockSpec((1, tile // MOBA_BLOCK, N_HEADS, height, MOBA_BLOCK), lambda i, j: (i, j, 0, 0, 0))

    return pl.pallas_call(
        functools.partial(_mix_in_kernel, conv_width=cw, attn_width=aw),
        grid=(b, s // tile),
        in_specs=[rows(d), _layer(gains.shape, layer), _layer(w_in.shape, layer), _layer(conv_w.shape, layer),
                  _layer(w_conv_out.shape, layer), rows(LANES), rows(LANES)],
        out_specs=[per_block(LANES), rows(aw), per_block(V_ROWS), rows(d), rows(d)],
        out_shape=[jax.ShapeDtypeStruct((b, n_blocks, N_HEADS, LANES, MOBA_BLOCK), BF16),
                   jax.ShapeDtypeStruct((b, s, aw), BF16),
                   jax.ShapeDtypeStruct((b, n_blocks, N_HEADS, V_ROWS, MOBA_BLOCK), BF16),
                   jax.ShapeDtypeStruct((b, s, d), BF16), jax.ShapeDtypeStruct((b, s, d), BF16)],
        scratch_shapes=[pltpu.VMEM((tile + SUBLANES, cw), F32)],
        compiler_params=_params("parallel", "arbitrary"),
        name="mix_in",
    )(h, gains, w_in, conv_w, w_conv_out, cos, sin)


def _moba_kernel(qt_ref, k_ref, vt_ref, o_ref, kmean_ref, selb_ref, s0_ref, stat0_ref, s1_ref, stat1_ref, m_ref,
                 l_ref, acc_ref, *, n_blocks):
    pair = pl.program_id(1)
    blk = MOBA_BLOCK
    groups = N_HEADS // HEADS_PER_GROUP

    @pl.when(pair == 0)
    def _():
        for n in range(n_blocks):
            rows = slice(n * blk, (n + 1) * blk)
            kmean_ref[n:n + 1, :] = jnp.mean(k_ref[0, rows, :].astype(F32), axis=0, keepdims=True)

    blk_row = lax.broadcasted_iota(jnp.int32, (n_blocks, 1), 0)
    blk_row_f = blk_row.astype(F32)

    def select_blocks(tile, qi):
        is_past = blk_row < qi
        for g in range(groups):
            kmean = kmean_ref[:, g * LANES:(g + 1) * LANES]
            kmean_hi = kmean.astype(BF16)
            kmean_lo = (kmean - kmean_hi.astype(F32)).astype(BF16)
            for hh in range(HEADS_PER_GROUP):
                h = g * HEADS_PER_GROUP + hh
                q_th = qt_ref[0, tile, h]
                gate = _dot(kmean_hi, q_th) + _dot(kmean_lo, q_th)
                gate = jnp.where(is_past, gate, -jnp.inf)
                keep = jnp.full((n_blocks, blk), MASKED, F32)
                for _ in range(TOP_K):
                    best = jnp.max(gate, axis=0, keepdims=True)
                    first = jnp.min(jnp.where(gate == best, blk_row_f, float(n_blocks)), axis=0, keepdims=True)
                    pick = blk_row_f == first
                    keep = jnp.where(pick, 0.0, keep)
                    gate = jnp.where(pick, -jnp.inf, gate)
                selb_ref[tile, h] = jnp.where(is_past, keep, MASKED)

    slots = ((s0_ref, stat0_ref), (s1_ref, stat1_ref))

    def score_stage(tile, h, n, slot, diag):
        s_ref, stat_ref = slots[slot]
        g = h // HEADS_PER_GROUP
        start = pl.multiple_of(n * blk, blk)
        s_t = _dot(k_ref[0, pl.ds(start, blk), g * LANES:(g + 1) * LANES], qt_ref[0, tile, h])
        if diag:
            key = lax.broadcasted_iota(jnp.int32, (blk, blk), 0)
            qry = lax.broadcasted_iota(jnp.int32, (blk, blk), 1)
            s_t = jnp.where(key <= qry, s_t, MASKED)
            keep = jnp.zeros((1, blk), F32)
        else:
            keep = selb_ref[tile, h, pl.ds(n, 1), :]
        m_old = m_ref[tile, h]
        m_new = jnp.maximum(m_old, jnp.max(s_t, axis=0, keepdims=True) + keep)
        s_ref[h] = s_t
        m_ref[tile, h] = m_new
        stat_ref[h, 0:1, :] = m_new
        stat_ref[h, 1:2, :] = jnp.exp2(m_old - m_new)
        stat_ref[h, 2:3, :] = keep

    def value_stage(tile, h, n, slot):
        s_ref, stat_ref = slots[slot]
        shift = stat_ref[h, 0:1, :]
        alpha = stat_ref[h, 1:2, :]
        keep = stat_ref[h, 2:3, :]
        p_t = jnp.exp2(s_ref[h] - shift).astype(BF16)
        pv_t = _dot(vt_ref[0, n, h], p_t)
        pv_t = jnp.where(keep == 0.0, pv_t, 0.0)
        acc_ref[tile, h] = alpha * acc_ref[tile, h] + pv_t[:HEAD_DIM]
        l_ref[tile, h] = alpha * l_ref[tile, h] + pv_t[HEAD_DIM:HEAD_DIM + 1]

    def reset(tile):
        for h in range(N_HEADS):
            m_ref[tile, h] = jnp.full((1, blk), MASKED, F32)
            l_ref[tile, h] = jnp.zeros((1, blk), F32)
            acc_ref[tile, h] = jnp.zeros((HEAD_DIM, blk), F32)

    def step(tile, qi, n, waiting):
        prev = jnp.where(n == 0, qi, n - 1)
        for h in range(N_HEADS):
            score_stage(tile, h, n, 1 - waiting, False)
            value_stage(tile, h, prev, waiting)

    def write_out(tile):
        for g in range(groups):
            heads = [g * HEADS_PER_GROUP + hh for hh in range(HEADS_PER_GROUP)]
            o_t = jnp.concatenate([acc_ref[tile, h] / l_ref[tile, h] for h in heads], axis=0)
            o_ref[0, tile * blk:(tile + 1) * blk, g * LANES:(g + 1) * LANES] = o_t.T.astype(o_ref.dtype)

    q_even = 2 * pair
    reset(0)
    select_blocks(0, q_even)
    for h in range(N_HEADS):
        score_stage(0, h, q_even, 0, True)

    def two_steps_even(t, carry):
        step(0, q_even, 2 * t, 0)
        step(0, q_even, 2 * t + 1, 1)
        return carry

    lax.fori_loop(0, pair, two_steps_even, 0)

    q_odd = q_even + 1
    last_even = jnp.where(q_even == 0, q_even, q_even - 1)
    reset(1)
    select_blocks(1, q_odd)
    for h in range(N_HEADS):
        value_stage(0, h, last_even, 0)
        score_stage(1, h, q_odd, 1, True)
    write_out(0)

    def two_steps_odd(t, carry):
        step(1, q_odd, 2 * t, 1)
        step(1, q_odd, 2 * t + 1, 0)
        return carry

    lax.fori_loop(0, pair, two_steps_odd, 0)
    step(1, q_odd, q_even, 1)
    for h in range(N_HEADS):
        value_stage(1, h, q_even, 0)
    write_out(1)


def _moba(qt, k, vt):
    b, s, aw = k.shape
    n_blocks = s // MOBA_BLOCK
    blk = MOBA_BLOCK
    tiles = 2
    assert n_blocks % tiles == 0
    return pl.pallas_call(
        functools.partial(_moba_kernel, n_blocks=n_blocks),
        grid=(b, n_blocks // tiles),
        in_specs=[pl.BlockSpec((1, tiles) + qt.shape[2:], lambda i, j: (i, j, 0, 0, 0)),
                  pl.BlockSpec((1, s, aw), lambda i, j: (i, 0, 0)),
                  pl.BlockSpec((1,) + vt.shape[1:], lambda i, j: (i, 0, 0, 0, 0))],
        out_specs=pl.BlockSpec((1, tiles * blk, aw), lambda i, j: (i, j, 0)),
        out_shape=jax.ShapeDtypeStruct((b, s, aw), BF16),
        scratch_shapes=[
            pltpu.VMEM((n_blocks, aw), F32),
            pltpu.VMEM((tiles, N_HEADS, n_blocks, blk), F32),
            pltpu.VMEM((N_HEADS, blk, blk), F32),
            pltpu.VMEM((N_HEADS, SUBLANES, blk), F32),
            pltpu.VMEM((N_HEADS, blk, blk), F32),
            pltpu.VMEM((N_HEADS, SUBLANES, blk), F32),
            pltpu.VMEM((tiles, N_HEADS, 1, blk), F32),
            pltpu.VMEM((tiles, N_HEADS, 1, blk), F32),
            pltpu.VMEM((tiles, N_HEADS, HEAD_DIM, blk), F32),
        ],
        compiler_params=_params("parallel", "arbitrary"),
        name="moba",
    )(qt, k, vt)


def kernel(x, p, positions, w_in, conv_w, w_conv_out, w_attn_out, w_o, ffn1_gate, ffn1_up, ffn1_down,
           ffn2_gate, ffn2_up, ffn2_down, norm_ffn1, norm_mix, norm_ffn2, norm_ple, w_ple_gate, w_ple_proj,
           norm_final):
    b, s, d = x.shape
    depth = w_in.shape[0]
    t = b * s
    assert s % MOBA_BLOCK == 0 and s % TOKEN_TILE == 0

    freq = jnp.arange(HALF_DIM, dtype=F32) / HALF_DIM
    inv_freq = jnp.tile(ROPE_THETA ** (-freq), LANES // HALF_DIM).reshape(1, LANES)
    cos, sin = _rope_tables(positions, inv_freq)

    def gains(g):
        return g.reshape(depth, 1, d)

    p_rows = p.reshape(depth, t, p.shape[-1])
    h = x.reshape(t, d)
    for i in range(depth):
        h = _ffn(h, gains(norm_ffn1), ffn1_gate, ffn1_up, ffn1_down, i)
        qt, k, vt, mc, ga = _mix_in(h.reshape(b, s, d), gains(norm_mix), w_in, conv_w, w_conv_out, cos, sin, i)
        o = _moba(qt, k, vt)
        h = _mix_ffn_ple(h, mc.reshape(t, d), ga.reshape(t, d), o.reshape(t, -1), p_rows, gains(norm_ffn2),
                         gains(norm_ple), norm_final.reshape(1, d), w_attn_out, w_o, ffn2_gate, ffn2_up,
                         ffn2_down, w_ple_gate, w_ple_proj, i, i == depth - 1)
    return h.reshape(b, s, d)
```

```python
import functools

import jax
import jax.numpy as jnp
from jax import lax
from jax.experimental import pallas as pl
from jax.experimental.pallas import tpu as pltpu

N_HEADS = 8
HEAD_DIM = 64
HALF_DIM = HEAD_DIM // 2
MOBA_BLOCK = 256
TOP_K = 3
MOBA_TILES = 4
ROPE_THETA = 10000.0
EPS = 1e-6

LANES = 128
SUBLANES = 8
HEADS_PER_GROUP = LANES // HEAD_DIM
BF16_SUBLANES = 2 * SUBLANES
V_ROWS = HEAD_DIM + BF16_SUBLANES
VMEM_LIMIT_BYTES = 56 * 1024 * 1024
MASKED = -1e30
LOG2_E = 1.4426950408889634

TOKEN_TILE = 512
STAGE_ROWS = 256
STAGE_DEPTH = 5
FF_CHUNK = 256
OUT_CHUNK = 256
PROJ_CHUNK = 512

F32 = jnp.float32
BF16 = jnp.bfloat16


def _dot(a, b):
    return jnp.dot(a, b, preferred_element_type=F32)


def _dot_nt(a, b):
    return lax.dot_general(a, b, (((1,), (1,)), ((), ())), preferred_element_type=F32)


def _rmsnorm(x, gain):
    ms = jnp.mean(x * x, axis=-1, keepdims=True)
    return x * lax.rsqrt(ms + EPS) * gain


def _prenorm(x, gain):
    r = lax.rsqrt(jnp.mean(x * x, axis=-1, keepdims=True) + EPS)
    return (x * gain).astype(BF16), r


def _sigmoid(x):
    return 1.0 / (1.0 + jnp.exp(-x))


def _resident(shape):
    nd = len(shape)
    return pl.BlockSpec(shape, lambda *_: (0,) * nd, pipeline_mode=pl.Buffered(1))


def _layer(stacked_shape, layer):
    nd = len(stacked_shape) - 1
    return pl.BlockSpec((None,) + tuple(stacked_shape[1:]), lambda *_: (layer,) + (0,) * nd,
                        pipeline_mode=pl.Buffered(1))


def _bf16(ref, rows=slice(None), cols=slice(None)):
    return ref[rows, cols].astype(BF16)


def _params(*semantics):
    return pltpu.CompilerParams(dimension_semantics=semantics, vmem_limit_bytes=VMEM_LIMIT_BYTES)


def _rope_table_kernel(pos_ref, inv_freq_ref, cos_ref, sin_ref):
    ang = pos_ref[0].astype(F32) * inv_freq_ref[...]
    lane = lax.broadcasted_iota(jnp.int32, (1, LANES), 1)
    first_half = (lane % HEAD_DIM) < HALF_DIM
    cos_ref[0] = jnp.cos(ang)
    sin = jnp.sin(ang)
    sin_ref[0] = jnp.where(first_half, -sin, sin)


def _rope_tables(positions, inv_freq):
    b, s = positions.shape
    tile = min(TOKEN_TILE, s)
    spec = pl.BlockSpec((1, tile, LANES), lambda i, j: (i, j, 0))
    return pl.pallas_call(
        _rope_table_kernel,
        grid=(b, s // tile),
        in_specs=[pl.BlockSpec((1, tile, 1), lambda i, j: (i, j, 0)), _resident((1, LANES))],
        out_specs=[spec, spec],
        out_shape=[jax.ShapeDtypeStruct((b, s, LANES), F32)] * 2,
        compiler_params=_params("parallel", "parallel"),
        name="rope_tables",
    )(positions.reshape(b, s, 1), inv_freq)


def _ffn_rows(h_ref, gain_ref, wg_ref, wu_ref, wd_ref, o_ref, act_ref):
    x = h_ref[...]
    n, r = _prenorm(x, gain_ref[...])
    d_ff = wg_ref.shape[1]
    for c in range(d_ff // FF_CHUNK):
        cols = slice(c * FF_CHUNK, (c + 1) * FF_CHUNK)
        g = _dot(n, _bf16(wg_ref, cols=cols)) * r
        u = _dot(n, _bf16(wu_ref, cols=cols)) * r
        act_ref[:, cols] = (g * _sigmoid(g) * u).astype(BF16)
    for c in range(x.shape[1] // OUT_CHUNK):
        cols = slice(c * OUT_CHUNK, (c + 1) * OUT_CHUNK)
        o_ref[:, cols] = x[:, cols] + 0.5 * _dot(act_ref[...], _bf16(wd_ref, cols=cols))


def _ffn(h, gains, wg, wu, wd, layer):
    t, d = h.shape
    d_ff = wg.shape[2]
    tile = min(TOKEN_TILE, t)
    row = pl.BlockSpec((tile, d), lambda i: (i, 0))
    return pl.pallas_call(
        _ffn_rows,
        grid=(t // tile,),
        in_specs=[row, _layer(gains.shape, layer), _layer(wg.shape, layer), _layer(wu.shape, layer),
                  _layer(wd.shape, layer)],
        out_specs=row,
        out_shape=jax.ShapeDtypeStruct((t, d), F32),
        scratch_shapes=[pltpu.VMEM((tile, d_ff), BF16)],
        compiler_params=_params("parallel"),
        name="ffn",
    )(h, gains, wg, wu, wd)


def _col_windows(src_hbm, layer, dst_ref, width):
    return [(src_hbm.at[layer, :, c:c + width], dst_ref.at[:, c:c + width])
            for c in range(0, dst_ref.shape[1], width)]


def _row_windows(src_hbm, layer, dst_ref, height):
    return [(src_hbm.at[layer, r:r + height, :], dst_ref.at[r:r + height, :])
            for r in range(0, dst_ref.shape[0], height)]


def _stage_params(jobs, ring_ref, sem_ref):
    depth = ring_ref.shape[0]

    def copy(i):
        return pltpu.make_async_copy(jobs[i][0], ring_ref.at[i % depth], sem_ref.at[i % depth])

    for i in range(min(depth - 1, len(jobs))):
        copy(i).start()
    for i, (_, dst) in enumerate(jobs):
        if i + depth - 1 < len(jobs):
            copy(i + depth - 1).start()
        copy(i).wait()
        dst[...] = ring_ref[i % depth].astype(BF16)


def _mix_ffn_ple_kernel(h_ref, mc_ref, ga_ref, attn_ref, p_ref, gain_ref, ple_gain_ref, final_gain_ref,
                        w_attn_out_hbm, w_o_hbm, wg_hbm, wu_hbm, wd_hbm, w_gate_hbm, w_proj_hbm, o_ref,
                        w_attn_out_ref, w_o_ref, wg_ref, wu_ref, wd_ref, w_gate_ref, w_proj_ref,
                        col_ring, row_ring, col_sem, row_sem, act_ref, x_ref, *, layer, final):
    @pl.when(pl.program_id(0) == 0)
    def _():
        row_jobs = (_row_windows(w_attn_out_hbm, layer, w_attn_out_ref, STAGE_ROWS)
                    + _row_windows(wd_hbm, layer, wd_ref, STAGE_ROWS)
                    + _row_windows(w_proj_hbm, layer, w_proj_ref, STAGE_ROWS))
        col_jobs = _col_windows(w_o_hbm, layer, w_o_ref, FF_CHUNK)
        for gate_job, up_job in zip(_col_windows(wg_hbm, layer, wg_ref, FF_CHUNK),
                                    _col_windows(wu_hbm, layer, wu_ref, FF_CHUNK)):
            col_jobs += [gate_job, up_job]
        col_jobs += _col_windows(w_gate_hbm, layer, w_gate_ref, FF_CHUNK)
        _stage_params(row_jobs, row_ring, row_sem)
        _stage_params(col_jobs, col_ring, col_sem)

    y_attn = _dot(attn_ref[...], w_attn_out_ref[...])
    merged = mc_ref[...].astype(F32) + ga_ref[...].astype(F32) * y_attn
    x_ref[...] = h_ref[...] + _dot(merged.astype(BF16), w_o_ref[...])
    _ffn_rows(x_ref, gain_ref, wg_ref, wu_ref, wd_ref, o_ref, act_ref)
    y = o_ref[...]
    n, r = _prenorm(y, ple_gain_ref[...])
    gate = _sigmoid(_dot(n, w_gate_ref[...]) * r)
    y = y + gate * _dot(p_ref[...].astype(BF16), w_proj_ref[...])
    o_ref[...] = _rmsnorm(y, final_gain_ref[...]) if final else y


def _mix_ffn_ple(h, mc, ga, attn, p, gains, ple_gains, final_gain, w_attn_out, w_o, wg, wu, wd, w_gate, w_proj,
                 layer, final):
    t, d = h.shape
    d_ff = wg.shape[2]
    aw = attn.shape[1]
    pd = p.shape[2]
    tile = min(TOKEN_TILE, t)

    def rows(width):
        return pl.BlockSpec((tile, width), lambda i: (i, 0))

    hbm = pl.BlockSpec(memory_space=pl.ANY)
    return pl.pallas_call(
        functools.partial(_mix_ffn_ple_kernel, layer=layer, final=final),
        grid=(t // tile,),
        in_specs=[rows(d), rows(d), rows(d), rows(aw), pl.BlockSpec((None, tile, pd), lambda i: (layer, i, 0)),
                  _layer(gains.shape, layer), _layer(ple_gains.shape, layer), _resident(final_gain.shape)]
                 + [hbm] * 7,
        out_specs=rows(d),
        out_shape=jax.ShapeDtypeStruct((t, d), F32),
        scratch_shapes=[
            pltpu.VMEM((aw, d), BF16), pltpu.VMEM((d, d), BF16), pltpu.VMEM((d, d_ff), BF16),
            pltpu.VMEM((d, d_ff), BF16), pltpu.VMEM((d_ff, d), BF16), pltpu.VMEM((d, d), BF16),
            pltpu.VMEM((pd, d), BF16),
            pltpu.VMEM((STAGE_DEPTH, d, FF_CHUNK), F32), pltpu.VMEM((STAGE_DEPTH, STAGE_ROWS, d), F32),
            pltpu.SemaphoreType.DMA((STAGE_DEPTH,)), pltpu.SemaphoreType.DMA((STAGE_DEPTH,)),
            pltpu.VMEM((tile, d_ff), BF16), pltpu.VMEM((tile, d), F32),
        ],
        compiler_params=_params("arbitrary"),
        name="mix_ffn_ple",
    )(h, mc, ga, attn, p, gains, ple_gains, final_gain, w_attn_out, w_o, wg, wu, wd, w_gate, w_proj)


def _mix_in_kernel(h_ref, gain_ref, w_in_ref, conv_w_ref, w_conv_out_ref, cos_ref, sin_ref,
                   qt_ref, k_ref, vt_ref, mc_ref, ga_ref, u_ref, *, conv_width, attn_width):
    tile = h_ref.shape[1]
    cw, aw = conv_width, attn_width
    d = h_ref.shape[2]
    @pl.when(pl.program_id(1) == 0)
    def _():
        u_ref[0:SUBLANES, :] = jnp.zeros((SUBLANES, cw), F32)

    n, r = _prenorm(h_ref[0], gain_ref[...])

    def proj(col0, width):
        return _dot(n, _bf16(w_in_ref, cols=slice(col0, col0 + width))) * r

    b_gate, c_gate, xc = proj(0, cw), proj(cw, cw), proj(2 * cw, cw)
    u = c_gate * xc
    u_ref[SUBLANES:SUBLANES + tile, :] = u

    cos = cos_ref[0]
    sin = sin_ref[0]
    lane = lax.broadcasted_iota(jnp.int32, (1, LANES), 1)
    first_half = (lane % HEAD_DIM) < HALF_DIM
    scale = HEAD_DIM ** -0.5 * LOG2_E
    dim_row = lax.broadcasted_iota(jnp.int32, (LANES, 1), 0)
    blocks = range(tile // MOBA_BLOCK)

    def rotary(xg):
        partner = jnp.where(first_half, pltpu.roll(xg, LANES - HALF_DIM, 1), pltpu.roll(xg, HALF_DIM, 1))
        return xg * cos + partner * sin

    q_rows = proj(3 * cw, aw)
    for g in range(aw // LANES):
        q_t = (rotary(q_rows[:, g * LANES:(g + 1) * LANES]) * scale).T
        for hh in range(HEADS_PER_GROUP):
            q_th = jnp.where((dim_row // HEAD_DIM) == hh, q_t, 0.0).astype(qt_ref.dtype)
            for j in blocks:
                qt_ref[0, j, g * HEADS_PER_GROUP + hh] = q_th[:, j * MOBA_BLOCK:(j + 1) * MOBA_BLOCK]
    k_rows = proj(3 * cw + aw, aw)
    for g in range(aw // LANES):
        cols = slice(g * LANES, (g + 1) * LANES)
        k_ref[0, :, cols] = rotary(k_rows[:, cols]).astype(k_ref.dtype)
    v_rows = proj(3 * cw + 2 * aw, aw)
    ones = jnp.ones((V_ROWS - HEAD_DIM, MOBA_BLOCK), vt_ref.dtype)
    for g in range(aw // LANES):
        v_t = v_rows[:, g * LANES:(g + 1) * LANES].T.astype(vt_ref.dtype)
        for hh in range(HEADS_PER_GROUP):
            for j in blocks:
                head = g * HEADS_PER_GROUP + hh
                vt_ref[0, j, head, 0:HEAD_DIM, :] = v_t[hh * HEAD_DIM:(hh + 1) * HEAD_DIM,
                                                        j * MOBA_BLOCK:(j + 1) * MOBA_BLOCK]
                vt_ref[0, j, head, HEAD_DIM:V_ROWS, :] = ones
    base = 3 * cw + 3 * aw
    for c in range(d // PROJ_CHUNK):
        cols = slice(c * PROJ_CHUNK, (c + 1) * PROJ_CHUNK)
        ga_ref[0, :, cols] = _sigmoid(proj(base + d + c * PROJ_CHUNK, PROJ_CHUNK)).astype(ga_ref.dtype)
    conv_gate = [_sigmoid(proj(base + c * PROJ_CHUNK, PROJ_CHUNK)) for c in range(d // PROJ_CHUNK)]

    u_prev1 = u_ref[SUBLANES - 1:SUBLANES - 1 + tile, :]
    u_prev2 = u_ref[SUBLANES - 2:SUBLANES - 2 + tile, :]
    conv_w = conv_w_ref[...]
    conv = u_prev2 * conv_w[0:1, :] + u_prev1 * conv_w[1:2, :] + u * conv_w[2:3, :]
    u_ref[0:SUBLANES, :] = u[tile - SUBLANES:tile, :]
    gated = (b_gate * conv).astype(BF16)
    for c in range(d // PROJ_CHUNK):
        cols = slice(c * PROJ_CHUNK, (c + 1) * PROJ_CHUNK)
        y_conv = _dot(gated, _bf16(w_conv_out_ref, cols=cols))
        mc_ref[0, :, cols] = (conv_gate[c] * y_conv).astype(mc_ref.dtype)


def _mix_in(h, gains, w_in, conv_w, w_conv_out, cos, sin, layer):
    b, s, d = h.shape
    cw = conv_w.shape[2]
    aw = N_HEADS * HEAD_DIM
    tile = min(TOKEN_TILE, s)

    n_blocks = s // MOBA_BLOCK

    def rows(width):
        return pl.BlockSpec((1, tile, width), lambda i, j: (i, j, 0))

    def per_block(height):
        return pl.BlockSpec((1, tile // MOBA_BLOCK, N_HEADS, height, MOBA_BLOCK), lambda i, j: (i, j, 0, 0, 0))

    return pl.pallas_call(
        functools.partial(_mix_in_kernel, conv_width=cw, attn_width=aw),
        grid=(b, s // tile),
        in_specs=[rows(d), _layer(gains.shape, layer), _layer(w_in.shape, layer), _layer(conv_w.shape, layer),
                  _layer(w_conv_out.shape, layer), rows(LANES), rows(LANES)],
        out_specs=[per_block(LANES), rows(aw), per_block(V_ROWS), rows(d), rows(d)],
        out_shape=[jax.ShapeDtypeStruct((b, n_blocks, N_HEADS, LANES, MOBA_BLOCK), BF16),
                   jax.ShapeDtypeStruct((b, s, aw), BF16),
                   jax.ShapeDtypeStruct((b, n_blocks, N_HEADS, V_ROWS, MOBA_BLOCK), BF16),
                   jax.ShapeDtypeStruct((b, s, d), BF16), jax.ShapeDtypeStruct((b, s, d), BF16)],
        scratch_shapes=[pltpu.VMEM((tile + SUBLANES, cw), F32)],
        compiler_params=_params("parallel", "arbitrary"),
        name="mix_in",
    )(h, gains, w_in, conv_w, w_conv_out, cos, sin)


def _moba_kernel(qt_ref, k_ref, vt_ref, o_ref, kmean_ref, selb_ref, s0_ref, stat0_ref, s1_ref, stat1_ref, m_ref,
                 l_ref, acc_ref, *, n_blocks):
    group = pl.program_id(1)
    blk = MOBA_BLOCK
    groups = N_HEADS // HEADS_PER_GROUP

    @pl.when(group == 0)
    def _():
        for n in range(n_blocks):
            rows = slice(n * blk, (n + 1) * blk)
            kmean_ref[n:n + 1, :] = jnp.mean(k_ref[0, rows, :].astype(F32), axis=0, keepdims=True)

    blk_row = lax.broadcasted_iota(jnp.int32, (n_blocks, 1), 0)
    blk_row_f = blk_row.astype(F32)

    def select_blocks(tile, qi):
        is_past = blk_row < qi
        for g in range(groups):
            kmean = kmean_ref[:, g * LANES:(g + 1) * LANES]
            kmean_hi = kmean.astype(BF16)
            kmean_lo = (kmean - kmean_hi.astype(F32)).astype(BF16)
            for hh in range(HEADS_PER_GROUP):
                h = g * HEADS_PER_GROUP + hh
                q_th = qt_ref[0, tile, h]
                gate = _dot(kmean_hi, q_th) + _dot(kmean_lo, q_th)
                gate = jnp.where(is_past, gate, -jnp.inf)
                keep = jnp.full((n_blocks, blk), MASKED, F32)
                for _ in range(TOP_K):
                    best = jnp.max(gate, axis=0, keepdims=True)
                    first = jnp.min(jnp.where(gate == best, blk_row_f, float(n_blocks)), axis=0, keepdims=True)
                    pick = blk_row_f == first
                    keep = jnp.where(pick, 0.0, keep)
                    gate = jnp.where(pick, -jnp.inf, gate)
                selb_ref[tile, h] = jnp.where(is_past, keep, MASKED)

    slots = ((s0_ref, stat0_ref), (s1_ref, stat1_ref))

    def score_stage(tile, h, n, slot, diag):
        s_ref, stat_ref = slots[slot]
        g = h // HEADS_PER_GROUP
        start = pl.multiple_of(n * blk, blk)
        s_t = _dot(k_ref[0, pl.ds(start, blk), g * LANES:(g + 1) * LANES], qt_ref[0, tile, h])
        if diag:
            key = lax.broadcasted_iota(jnp.int32, (blk, blk), 0)
            qry = lax.broadcasted_iota(jnp.int32, (blk, blk), 1)
            s_t = jnp.where(key <= qry, s_t, MASKED)
            keep = jnp.zeros((1, blk), F32)
        else:
            keep = selb_ref[tile, h, pl.ds(n, 1), :]
        m_old = m_ref[tile, h]
        m_new = jnp.maximum(m_old, jnp.max(s_t, axis=0, keepdims=True) + keep)
        s_ref[h] = s_t
        m_ref[tile, h] = m_new
        stat_ref[h, 0:1, :] = m_new
        stat_ref[h, 1:2, :] = jnp.exp2(m_old - m_new)
        stat_ref[h, 2:3, :] = keep

    def value_stage(tile, h, n, slot):
        s_ref, stat_ref = slots[slot]
        shift = stat_ref[h, 0:1, :]
        alpha = stat_ref[h, 1:2, :]
        keep = stat_ref[h, 2:3, :]
        p_t = jnp.exp2(s_ref[h] - shift).astype(BF16)
        pv_t = _dot(vt_ref[0, n, h], p_t)
        pv_t = jnp.where(keep == 0.0, pv_t, 0.0)
        acc_ref[tile, h] = alpha * acc_ref[tile, h] + pv_t[:HEAD_DIM]
        l_ref[tile, h] = alpha * l_ref[tile, h] + pv_t[HEAD_DIM:HEAD_DIM + 1]

    def reset(tile):
        for h in range(N_HEADS):
            m_ref[tile, h] = jnp.full((1, blk), MASKED, F32)
            l_ref[tile, h] = jnp.zeros((1, blk), F32)
            acc_ref[tile, h] = jnp.zeros((HEAD_DIM, blk), F32)

    def step(tile, qi, n, waiting):
        prev = jnp.where(n == 0, qi, n - 1)
        for h in range(N_HEADS):
            score_stage(tile, h, n, 1 - waiting, False)
            value_stage(tile, h, prev, waiting)

    def write_out(tile):
        for g in range(groups):
            heads = [g * HEADS_PER_GROUP + hh for hh in range(HEADS_PER_GROUP)]
            o_t = jnp.concatenate([acc_ref[tile, h] / l_ref[tile, h] for h in heads], axis=0)
            o_ref[0, tile * blk:(tile + 1) * blk, g * LANES:(g + 1) * LANES] = o_t.T.astype(o_ref.dtype)

    base = MOBA_TILES * group
    pending = None
    first_slot = 0
    for tile in range(MOBA_TILES):
        qi = base + tile
        reset(tile)
        select_blocks(tile, qi)
        for h in range(N_HEADS):
            if pending is not None:
                value_stage(pending[0], h, pending[1], pending[2])
            score_stage(tile, h, qi, first_slot, True)
        if pending is not None:
            write_out(pending[0])

        def full_trip(t, carry, tile=tile, qi=qi, first_slot=first_slot):
            for i in range(MOBA_TILES):
                step(tile, qi, MOBA_TILES * t + i, (first_slot + i) % 2)
            return carry

        lax.fori_loop(0, group, full_trip, 0)
        for i in range(tile):
            step(tile, qi, base + i, (first_slot + i) % 2)
        last_slot = (first_slot + tile) % 2
        pending = (tile, jnp.where(qi == 0, qi, qi - 1), last_slot)
        first_slot = 1 - last_slot
    for h in range(N_HEADS):
        value_stage(pending[0], h, pending[1], pending[2])
    write_out(pending[0])


def _moba(qt, k, vt):
    b, s, aw = k.shape
    n_blocks = s // MOBA_BLOCK
    blk = MOBA_BLOCK
    tiles = MOBA_TILES
    assert n_blocks % tiles == 0 and tiles % 2 == 0
    return pl.pallas_call(
        functools.partial(_moba_kernel, n_blocks=n_blocks),
        grid=(b, n_blocks // tiles),
        in_specs=[pl.BlockSpec((1, tiles) + qt.shape[2:], lambda i, j: (i, j, 0, 0, 0)),
                  pl.BlockSpec((1, s, aw), lambda i, j: (i, 0, 0)),
                  pl.BlockSpec((1,) + vt.shape[1:], lambda i, j: (i, 0, 0, 0, 0))],
        out_specs=pl.BlockSpec((1, tiles * blk, aw), lambda i, j: (i, j, 0)),
        out_shape=jax.ShapeDtypeStruct((b, s, aw), BF16),
        scratch_shapes=[
            pltpu.VMEM((n_blocks, aw), F32),
            pltpu.VMEM((tiles, N_HEADS, n_blocks, blk), F32),
            pltpu.VMEM((N_HEADS, blk, blk), F32),
            pltpu.VMEM((N_HEADS, SUBLANES, blk), F32),
            pltpu.VMEM((N_HEADS, blk, blk), F32),
            pltpu.VMEM((N_HEADS, SUBLANES, blk), F32),
            pltpu.VMEM((tiles, N_HEADS, 1, blk), F32),
            pltpu.VMEM((tiles, N_HEADS, 1, blk), F32),
            pltpu.VMEM((tiles, N_HEADS, HEAD_DIM, blk), F32),
        ],
        compiler_params=_params("parallel", "arbitrary"),
        name="moba",
    )(qt, k, vt)


def kernel(x, p, positions, w_in, conv_w, w_conv_out, w_attn_out, w_o, ffn1_gate, ffn1_up, ffn1_down,
           ffn2_gate, ffn2_up, ffn2_down, norm_ffn1, norm_mix, norm_ffn2, norm_ple, w_ple_gate, w_ple_proj,
           norm_final):
    b, s, d = x.shape
    depth = w_in.shape[0]
    t = b * s
    assert s % MOBA_BLOCK == 0 and s % TOKEN_TILE == 0

    freq = jnp.arange(HALF_DIM, dtype=F32) / HALF_DIM
    inv_freq = jnp.tile(ROPE_THETA ** (-freq), LANES // HALF_DIM).reshape(1, LANES)
    cos, sin = _rope_tables(positions, inv_freq)

    def gains(g):
        return g.reshape(depth, 1, d)

    p_rows = p.reshape(depth, t, p.shape[-1])
    h = x.reshape(t, d)
    for i in range(depth):
        h = _ffn(h, gains(norm_ffn1), ffn1_gate, ffn1_up, ffn1_down, i)
        qt, k, vt, mc, ga = _mix_in(h.reshape(b, s, d), gains(norm_mix), w_in, conv_w, w_conv_out, cos, sin, i)
        o = _moba(qt, k, vt)
        h = _mix_ffn_ple(h, mc.reshape(t, d), ga.reshape(t, d), o.reshape(t, -1), p_rows, gains(norm_ffn2),
                         gains(norm_ple), norm_final.reshape(1, d), w_attn_out, w_o, ffn2_gate, ffn2_up,
                         ffn2_down, w_ple_gate, w_ple_proj, i, i == depth - 1)
    return h.reshape(b, s, d)
```

```python
import functools

import jax
import jax.numpy as jnp
from jax import lax
from jax.experimental import pallas as pl
from jax.experimental.pallas import tpu as pltpu

N_HEADS = 8
HEAD_DIM = 64
HALF_DIM = HEAD_DIM // 2
MOBA_BLOCK = 256
TOP_K = 3
MOBA_TILES = 4
ROPE_THETA = 10000.0
EPS = 1e-6

LANES = 128
SUBLANES = 8
HEADS_PER_GROUP = LANES // HEAD_DIM
BF16_SUBLANES = 2 * SUBLANES
V_ROWS = HEAD_DIM + BF16_SUBLANES
VMEM_LIMIT_BYTES = 56 * 1024 * 1024
MASKED = -1e30
LOG2_E = 1.4426950408889634

TOKEN_TILE = 512
STAGE_ROWS = 256
STAGE_DEPTH = 5
FF_CHUNK = 256
OUT_CHUNK = 256
PROJ_CHUNK = 512

F32 = jnp.float32
BF16 = jnp.bfloat16


def _dot(a, b):
    return jnp.dot(a, b, preferred_element_type=F32)


def _dot_nt(a, b):
    return lax.dot_general(a, b, (((1,), (1,)), ((), ())), preferred_element_type=F32)


def _rmsnorm(x, gain):
    ms = jnp.mean(x * x, axis=-1, keepdims=True)
    return x * lax.rsqrt(ms + EPS) * gain


def _prenorm(x, gain):
    r = lax.rsqrt(jnp.mean(x * x, axis=-1, keepdims=True) + EPS)
    return (x * gain).astype(BF16), r


def _sigmoid(x):
    return 1.0 / (1.0 + jnp.exp(-x))


def _resident(shape):
    nd = len(shape)
    return pl.BlockSpec(shape, lambda *_: (0,) * nd, pipeline_mode=pl.Buffered(1))


def _layer(stacked_shape, layer):
    nd = len(stacked_shape) - 1
    return pl.BlockSpec((None,) + tuple(stacked_shape[1:]), lambda *_: (layer,) + (0,) * nd,
                        pipeline_mode=pl.Buffered(1))


def _bf16(ref, rows=slice(None), cols=slice(None)):
    return ref[rows, cols].astype(BF16)


def _params(*semantics):
    return pltpu.CompilerParams(dimension_semantics=semantics, vmem_limit_bytes=VMEM_LIMIT_BYTES)


def _ffn_rows(h_ref, gain_ref, wg_ref, wu_ref, wd_ref, o_ref, act_ref):
    x = h_ref[...]
    n, r = _prenorm(x, gain_ref[...])
    d_ff = wg_ref.shape[1]
    for c in range(d_ff // FF_CHUNK):
        cols = slice(c * FF_CHUNK, (c + 1) * FF_CHUNK)
        g = _dot(n, _bf16(wg_ref, cols=cols)) * r
        u = _dot(n, _bf16(wu_ref, cols=cols)) * r
        act_ref[:, cols] = (g * _sigmoid(g) * u).astype(BF16)
    for c in range(x.shape[1] // OUT_CHUNK):
        cols = slice(c * OUT_CHUNK, (c + 1) * OUT_CHUNK)
        o_ref[:, cols] = x[:, cols] + 0.5 * _dot(act_ref[...], _bf16(wd_ref, cols=cols))


def _ffn_rope_kernel(h_ref, gain_ref, wg_ref, wu_ref, wd_ref, pos_ref, inv_freq_ref, o_ref, cos_ref, sin_ref,
                     act_ref):
    ang = pos_ref[...].astype(F32) * inv_freq_ref[...]
    lane = lax.broadcasted_iota(jnp.int32, (1, LANES), 1)
    first_half = (lane % HEAD_DIM) < HALF_DIM
    cos_ref[...] = jnp.cos(ang)
    sin = jnp.sin(ang)
    sin_ref[...] = jnp.where(first_half, -sin, sin)
    _ffn_rows(h_ref, gain_ref, wg_ref, wu_ref, wd_ref, o_ref, act_ref)


def _ffn(h, gains, wg, wu, wd, layer, rope=None):
    t, d = h.shape
    d_ff = wg.shape[2]
    tile = min(TOKEN_TILE, t)

    def rows(width):
        return pl.BlockSpec((tile, width), lambda i: (i, 0))

    in_specs = [rows(d), _layer(gains.shape, layer), _layer(wg.shape, layer), _layer(wu.shape, layer),
                _layer(wd.shape, layer)]
    args = [h, gains, wg, wu, wd]
    out_specs, out_shape, body = rows(d), jax.ShapeDtypeStruct((t, d), F32), _ffn_rows
    if rope is not None:
        positions, inv_freq = rope
        in_specs += [rows(1), _resident(inv_freq.shape)]
        args += [positions, inv_freq]
        out_specs = [rows(d), rows(LANES), rows(LANES)]
        out_shape = [out_shape] + [jax.ShapeDtypeStruct((t, LANES), F32)] * 2
        body = _ffn_rope_kernel
    return pl.pallas_call(
        body,
        grid=(t // tile,),
        in_specs=in_specs,
        out_specs=out_specs,
        out_shape=out_shape,
        scratch_shapes=[pltpu.VMEM((tile, d_ff), BF16)],
        compiler_params=_params("parallel"),
        name="ffn" if rope is None else "ffn_rope",
    )(*args)


def _col_windows(src_hbm, layer, dst_ref, width):
    return [(src_hbm.at[layer, :, c:c + width], dst_ref.at[:, c:c + width])
            for c in range(0, dst_ref.shape[1], width)]


def _row_windows(src_hbm, layer, dst_ref, height):
    return [(src_hbm.at[layer, r:r + height, :], dst_ref.at[r:r + height, :])
            for r in range(0, dst_ref.shape[0], height)]


def _stage_params(jobs, ring_ref, sem_ref):
    depth = ring_ref.shape[0]

    def copy(i):
        return pltpu.make_async_copy(jobs[i][0], ring_ref.at[i % depth], sem_ref.at[i % depth])

    for i in range(min(depth - 1, len(jobs))):
        copy(i).start()
    for i, (_, dst) in enumerate(jobs):
        if i + depth - 1 < len(jobs):
            copy(i + depth - 1).start()
        copy(i).wait()
        dst[...] = ring_ref[i % depth].astype(BF16)


def _mix_ffn_ple_kernel(h_ref, mc_ref, ga_ref, attn_ref, p_ref, gain_ref, ple_gain_ref, final_gain_ref,
                        w_attn_out_hbm, w_o_hbm, wg_hbm, wu_hbm, wd_hbm, w_gate_hbm, w_proj_hbm, o_ref,
                        w_attn_out_ref, w_o_ref, wg_ref, wu_ref, wd_ref, w_gate_ref, w_proj_ref,
                        col_ring, row_ring, col_sem, row_sem, act_ref, x_ref, *, layer, final):
    @pl.when(pl.program_id(0) == 0)
    def _():
        row_jobs = (_row_windows(w_attn_out_hbm, layer, w_attn_out_ref, STAGE_ROWS)
                    + _row_windows(wd_hbm, layer, wd_ref, STAGE_ROWS)
                    + _row_windows(w_proj_hbm, layer, w_proj_ref, STAGE_ROWS))
        col_jobs = _col_windows(w_o_hbm, layer, w_o_ref, FF_CHUNK)
        for gate_job, up_job in zip(_col_windows(wg_hbm, layer, wg_ref, FF_CHUNK),
                                    _col_windows(wu_hbm, layer, wu_ref, FF_CHUNK)):
            col_jobs += [gate_job, up_job]
        col_jobs += _col_windows(w_gate_hbm, layer, w_gate_ref, FF_CHUNK)
        _stage_params(row_jobs, row_ring, row_sem)
        _stage_params(col_jobs, col_ring, col_sem)

    y_attn = _dot(attn_ref[...], w_attn_out_ref[...])
    merged = mc_ref[...].astype(F32) + ga_ref[...].astype(F32) * y_attn
    x_ref[...] = h_ref[...] + _dot(merged.astype(BF16), w_o_ref[...])
    _ffn_rows(x_ref, gain_ref, wg_ref, wu_ref, wd_ref, o_ref, act_ref)
    y = o_ref[...]
    n, r = _prenorm(y, ple_gain_ref[...])
    gate = _sigmoid(_dot(n, w_gate_ref[...]) * r)
    y = y + gate * _dot(p_ref[...].astype(BF16), w_proj_ref[...])
    o_ref[...] = _rmsnorm(y, final_gain_ref[...]) if final else y


def _mix_ffn_ple(h, mc, ga, attn, p, gains, ple_gains, final_gain, w_attn_out, w_o, wg, wu, wd, w_gate, w_proj,
                 layer, final):
    t, d = h.shape
    d_ff = wg.shape[2]
    aw = attn.shape[1]
    pd = p.shape[2]
    tile = min(TOKEN_TILE, t)

    def rows(width):
        return pl.BlockSpec((tile, width), lambda i: (i, 0))

    hbm = pl.BlockSpec(memory_space=pl.ANY)
    return pl.pallas_call(
        functools.partial(_mix_ffn_ple_kernel, layer=layer, final=final),
        grid=(t // tile,),
        in_specs=[rows(d), rows(d), rows(d), rows(aw), pl.BlockSpec((None, tile, pd), lambda i: (layer, i, 0)),
                  _layer(gains.shape, layer), _layer(ple_gains.shape, layer), _resident(final_gain.shape)]
                 + [hbm] * 7,
        out_specs=rows(d),
        out_shape=jax.ShapeDtypeStruct((t, d), F32),
        scratch_shapes=[
            pltpu.VMEM((aw, d), BF16), pltpu.VMEM((d, d), BF16), pltpu.VMEM((d, d_ff), BF16),
            pltpu.VMEM((d, d_ff), BF16), pltpu.VMEM((d_ff, d), BF16), pltpu.VMEM((d, d), BF16),
            pltpu.VMEM((pd, d), BF16),
            pltpu.VMEM((STAGE_DEPTH, d, FF_CHUNK), F32), pltpu.VMEM((STAGE_DEPTH, STAGE_ROWS, d), F32),
            pltpu.SemaphoreType.DMA((STAGE_DEPTH,)), pltpu.SemaphoreType.DMA((STAGE_DEPTH,)),
            pltpu.VMEM((tile, d_ff), BF16), pltpu.VMEM((tile, d), F32),
        ],
        compiler_params=_params("arbitrary"),
        name="mix_ffn_ple",
    )(h, mc, ga, attn, p, gains, ple_gains, final_gain, w_attn_out, w_o, wg, wu, wd, w_gate, w_proj)


def _mix_in_kernel(h_ref, gain_ref, w_in_ref, conv_w_ref, w_conv_out_ref, cos_ref, sin_ref,
                   qt_ref, k_ref, vt_ref, mc_ref, ga_ref, u_ref, *, conv_width, attn_width):
    tile = h_ref.shape[1]
    cw, aw = conv_width, attn_width
    d = h_ref.shape[2]
    @pl.when(pl.program_id(1) == 0)
    def _():
        u_ref[0:SUBLANES, :] = jnp.zeros((SUBLANES, cw), F32)

    n, r = _prenorm(h_ref[0], gain_ref[...])

    def proj(col0, width):
        return _dot(n, _bf16(w_in_ref, cols=slice(col0, col0 + width))) * r

    b_gate, c_gate, xc = proj(0, cw), proj(cw, cw), proj(2 * cw, cw)
    u = c_gate * xc
    u_ref[SUBLANES:SUBLANES + tile, :] = u

    cos = cos_ref[0]
    sin = sin_ref[0]
    lane = lax.broadcasted_iota(jnp.int32, (1, LANES), 1)
    first_half = (lane % HEAD_DIM) < HALF_DIM
    scale = HEAD_DIM ** -0.5 * LOG2_E
    dim_row = lax.broadcasted_iota(jnp.int32, (LANES, 1), 0)
    blocks = range(tile // MOBA_BLOCK)

    def rotary(xg):
        partner = jnp.where(first_half, pltpu.roll(xg, LANES - HALF_DIM, 1), pltpu.roll(xg, HALF_DIM, 1))
        return xg * cos + partner * sin

    q_rows, k_rows, v_rows = proj(3 * cw, aw), proj(3 * cw + aw, aw), proj(3 * cw + 2 * aw, aw)
    for g in range(aw // LANES):
        q_t = (rotary(q_rows[:, g * LANES:(g + 1) * LANES]) * scale).T
        for hh in range(HEADS_PER_GROUP):
            q_th = jnp.where((dim_row // HEAD_DIM) == hh, q_t, 0.0).astype(qt_ref.dtype)
            for j in blocks:
                qt_ref[0, j, g * HEADS_PER_GROUP + hh] = q_th[:, j * MOBA_BLOCK:(j + 1) * MOBA_BLOCK]
    for g in range(aw // LANES):
        cols = slice(g * LANES, (g + 1) * LANES)
        k_ref[0, :, cols] = rotary(k_rows[:, cols]).astype(k_ref.dtype)
    ones =jnp.ones((V_ROWS - HEAD_DIM, MOBA_BLOCK), vt_ref.dtype)
    for g in range(aw // LANES):
        v_t = v_rows[:, g * LANES:(g + 1) * LANES].T.astype(vt_ref.dtype)
        for hh in range(HEADS_PER_GROUP):
            for j in blocks:
                head = g * HEADS_PER_GROUP + hh
                vt_ref[0, j, head, 0:HEAD_DIM, :] = v_t[hh * HEAD_DIM:(hh + 1) * HEAD_DIM,
                                                        j * MOBA_BLOCK:(j + 1) * MOBA_BLOCK]
                vt_ref[0, j, head, HEAD_DIM:V_ROWS, :] = ones
    base = 3 * cw + 3 * aw
    for c in range(d // PROJ_CHUNK):
        cols = slice(c * PROJ_CHUNK, (c + 1) * PROJ_CHUNK)
        ga_ref[0, :, cols] = _sigmoid(proj(base + d + c * PROJ_CHUNK, PROJ_CHUNK)).astype(ga_ref.dtype)
    conv_gate = [_sigmoid(proj(base + c * PROJ_CHUNK, PROJ_CHUNK)) for c in range(d // PROJ_CHUNK)]

    u_prev1 = u_ref[SUBLANES - 1:SUBLANES - 1 + tile, :]
    u_prev2 = u_ref[SUBLANES - 2:SUBLANES - 2 + tile, :]
    conv_w = conv_w_ref[...]
    conv = u_prev2 * conv_w[0:1, :] + u_prev1 * conv_w[1:2, :] + u * conv_w[2:3, :]
    u_ref[0:SUBLANES, :] = u[tile - SUBLANES:tile, :]
    gated = (b_gate * conv).astype(BF16)
    for c in range(d // PROJ_CHUNK):
        cols = slice(c * PROJ_CHUNK, (c + 1) * PROJ_CHUNK)
        y_conv = _dot(gated, _bf16(w_conv_out_ref, cols=cols))
        mc_ref[0, :, cols] = (conv_gate[c] * y_conv).astype(mc_ref.dtype)


def _mix_in(h, gains, w_in, conv_w, w_conv_out, cos, sin, layer):
    b, s, d = h.shape
    cw = conv_w.shape[2]
    aw = N_HEADS * HEAD_DIM
    tile = min(TOKEN_TILE, s)

    n_blocks = s // MOBA_BLOCK

    def rows(width):
        return pl.BlockSpec((1, tile, width), lambda i, j: (i, j, 0))

    def per_block(height):
        return pl.BlockSpec((1, tile // MOBA_BLOCK, N_HEADS, height, MOBA_BLOCK), lambda i, j: (i, j, 0, 0, 0))

    return pl.pallas_call(
        functools.partial(_mix_in_kernel, conv_width=cw, attn_width=aw),
        grid=(b, s // tile),
        in_specs=[rows(d), _layer(gains.shape, layer), _layer(w_in.shape, layer), _layer(conv_w.shape, layer),
                  _layer(w_conv_out.shape, layer), rows(LANES), rows(LANES)],
        out_specs=[per_block(LANES), rows(aw), per_block(V_ROWS), rows(d), rows(d)],
        out_shape=[jax.ShapeDtypeStruct((b, n_blocks, N_HEADS, LANES, MOBA_BLOCK), BF16),
                   jax.ShapeDtypeStruct((b, s, aw), BF16),
                   jax.ShapeDtypeStruct((b, n_blocks, N_HEADS, V_ROWS, MOBA_BLOCK), BF16),
                   jax.ShapeDtypeStruct((b, s, d), BF16), jax.ShapeDtypeStruct((b, s, d), BF16)],
        scratch_shapes=[pltpu.VMEM((tile + SUBLANES, cw), F32)],
        compiler_params=_params("parallel", "arbitrary"),
        name="mix_in",
    )(h, gains, w_in, conv_w, w_conv_out, cos, sin)


def _moba_kernel(qt_ref, k_ref, vt_ref, o_ref, kmean_ref, selb_ref, s0_ref, stat0_ref, s1_ref, stat1_ref, m_ref,
                 l_ref, acc_ref, *, n_blocks):
    group = pl.program_id(1)
    blk = MOBA_BLOCK
    groups = N_HEADS // HEADS_PER_GROUP

    @pl.when(group == 0)
    def _():
        for n in range(n_blocks):
            rows = slice(n * blk, (n + 1) * blk)
            kmean_ref[n:n + 1, :] = jnp.mean(k_ref[0, rows, :].astype(F32), axis=0, keepdims=True)

    blk_row = lax.broadcasted_iota(jnp.int32, (n_blocks, 1), 0)
    blk_row_f = blk_row.astype(F32)

    def select_blocks(tile, qi):
        is_past = blk_row < qi
        for g in range(groups):
            kmean = kmean_ref[:, g * LANES:(g + 1) * LANES]
            kmean_hi = kmean.astype(BF16)
            kmean_lo = (kmean - kmean_hi.astype(F32)).astype(BF16)
            for hh in range(HEADS_PER_GROUP):
                h = g * HEADS_PER_GROUP + hh
                q_th = qt_ref[0, tile, h]
                gate = _dot(kmean_hi, q_th) + _dot(kmean_lo, q_th)
                gate = jnp.where(is_past, gate, -jnp.inf)
                keep = jnp.full((n_blocks, blk), MASKED, F32)
                for _ in range(TOP_K):
                    best = jnp.max(gate, axis=0, keepdims=True)
                    first = jnp.min(jnp.where(gate == best, blk_row_f, float(n_blocks)), axis=0, keepdims=True)
                    pick = blk_row_f == first
                    keep = jnp.where(pick, 0.0, keep)
                    gate = jnp.where(pick, -jnp.inf, gate)
                selb_ref[tile, h] = jnp.where(is_past, keep, MASKED)

    slots = ((s0_ref, stat0_ref), (s1_ref, stat1_ref))

    def score_stage(tile, h, n, slot, diag):
        s_ref, stat_ref = slots[slot]
        g = h // HEADS_PER_GROUP
        start = pl.multiple_of(n * blk, blk)
        s_t = _dot(k_ref[0, pl.ds(start, blk), g * LANES:(g + 1) * LANES], qt_ref[0, tile, h])
        if diag:
            key = lax.broadcasted_iota(jnp.int32, (blk, blk), 0)
            qry = lax.broadcasted_iota(jnp.int32, (blk, blk), 1)
            s_t = jnp.where(key <= qry, s_t, MASKED)
            keep = jnp.zeros((1, blk), F32)
        else:
            keep = selb_ref[tile, h, pl.ds(n, 1), :]
        m_old = m_ref[tile, h]
        m_new = jnp.maximum(m_old, jnp.max(s_t, axis=0, keepdims=True) + keep)
        s_ref[h] = s_t
        m_ref[tile, h] = m_new
        stat_ref[h, 0:1, :] = m_new
        stat_ref[h, 1:2, :] = jnp.exp2(m_old - m_new)
        stat_ref[h, 2:3, :] = keep

    def value_stage(tile, h, n, slot):
        s_ref, stat_ref = slots[slot]
        shift = stat_ref[h, 0:1, :]
        alpha = stat_ref[h, 1:2, :]
        keep = stat_ref[h, 2:3, :]
        p_t = jnp.exp2(s_ref[h] - shift).astype(BF16)
        pv_t = _dot(vt_ref[0, n, h], p_t)
        pv_t = jnp.where(keep == 0.0, pv_t, 0.0)
        acc_ref[tile, h] = alpha * acc_ref[tile, h] + pv_t[:HEAD_DIM]
        l_ref[tile, h] = alpha * l_ref[tile, h] + pv_t[HEAD_DIM:HEAD_DIM + 1]

    def reset(tile):
        for h in range(N_HEADS):
            m_ref[tile, h] = jnp.full((1, blk), MASKED, F32)
            l_ref[tile, h] = jnp.zeros((1, blk), F32)
            acc_ref[tile, h] = jnp.zeros((HEAD_DIM, blk), F32)

    def step(tile, qi, n, waiting):
        prev = jnp.where(n == 0, qi, n - 1)
        for h in range(N_HEADS):
            score_stage(tile, h, n, 1 - waiting, False)
            value_stage(tile, h, prev, waiting)

    def write_out(tile):
        for g in range(groups):
            heads = [g * HEADS_PER_GROUP + hh for hh in range(HEADS_PER_GROUP)]
            o_t = jnp.concatenate([acc_ref[tile, h] / l_ref[tile, h] for h in heads], axis=0)
            o_ref[0, tile * blk:(tile + 1) * blk, g * LANES:(g + 1) * LANES] = o_t.T.astype(o_ref.dtype)

    base = MOBA_TILES * group
    pending = None
    first_slot = 0
    for tile in range(MOBA_TILES):
        qi = base + tile
        reset(tile)
        select_blocks(tile, qi)
        for h in range(N_HEADS):
            if pending is not None:
                value_stage(pending[0], h, pending[1], pending[2])
            score_stage(tile, h, qi, first_slot, True)
        if pending is not None:
            write_out(pending[0])

        def full_trip(t, carry, tile=tile, qi=qi, first_slot=first_slot):
            for i in range(MOBA_TILES):
                step(tile, qi, MOBA_TILES * t + i, (first_slot + i) % 2)
            return carry

        lax.fori_loop(0, group, full_trip, 0)
        for i in range(tile):
            step(tile, qi, base + i, (first_slot + i) % 2)
        last_slot = (first_slot + tile) % 2
        pending = (tile, jnp.where(qi == 0, qi, qi - 1), last_slot)
        first_slot = 1 - last_slot
    for h in range(N_HEADS):
        value_stage(pending[0], h, pending[1], pending[2])
    write_out(pending[0])


def _moba(qt, k, vt):
    b, s, aw = k.shape
    n_blocks = s // MOBA_BLOCK
    blk = MOBA_BLOCK
    tiles = MOBA_TILES
    assert n_blocks % tiles == 0 and tiles % 2 == 0
    return pl.pallas_call(
        functools.partial(_moba_kernel, n_blocks=n_blocks),
        grid=(b, n_blocks // tiles),
        in_specs=[pl.BlockSpec((1, tiles) + qt.shape[2:], lambda i, j: (i, j, 0, 0, 0)),
                  pl.BlockSpec((1, s, aw), lambda i, j: (i, 0, 0)),
                  pl.BlockSpec((1,) + vt.shape[1:], lambda i, j: (i, 0, 0, 0, 0))],
        out_specs=pl.BlockSpec((1, tiles * blk, aw), lambda i, j: (i, j, 0)),
        out_shape=jax.ShapeDtypeStruct((b, s, aw), BF16),
        scratch_shapes=[
            pltpu.VMEM((n_blocks, aw), F32),
            pltpu.VMEM((tiles, N_HEADS, n_blocks, blk), F32),
            pltpu.VMEM((N_HEADS, blk, blk), F32),
            pltpu.VMEM((N_HEADS, SUBLANES, blk), F32),
            pltpu.VMEM((N_HEADS, blk, blk), F32),
            pltpu.VMEM((N_HEADS, SUBLANES, blk), F32),
            pltpu.VMEM((tiles, N_HEADS, 1, blk), F32),
            pltpu.VMEM((tiles, N_HEADS, 1, blk), F32),
            pltpu.VMEM((tiles, N_HEADS, HEAD_DIM, blk), F32),
        ],
        compiler_params=_params("parallel", "arbitrary"),
        name="moba",
    )(qt, k, vt)


def kernel(x, p, positions, w_in, conv_w, w_conv_out, w_attn_out, w_o, ffn1_gate, ffn1_up, ffn1_down,
           ffn2_gate, ffn2_up, ffn2_down, norm_ffn1, norm_mix, norm_ffn2, norm_ple, w_ple_gate, w_ple_proj,
           norm_final):
    b, s, d = x.shape
    depth = w_in.shape[0]
    t = b * s
    assert s % MOBA_BLOCK == 0 and s % TOKEN_TILE == 0

    freq = jnp.arange(HALF_DIM, dtype=F32) / HALF_DIM
    inv_freq = jnp.tile(ROPE_THETA ** (-freq), LANES // HALF_DIM).reshape(1, LANES)

    def gains(g):
        return g.reshape(depth, 1, d)

    p_rows = p.reshape(depth, t, p.shape[-1])
    h = x.reshape(t, d)
    for i in range(depth):
        if i == 0:
            h, cos, sin = _ffn(h, gains(norm_ffn1), ffn1_gate, ffn1_up, ffn1_down, i,
                               rope=(positions.reshape(t, 1), inv_freq))
            cos, sin = cos.reshape(b, s, LANES), sin.reshape(b, s, LANES)
        else:
            h = _ffn(h, gains(norm_ffn1), ffn1_gate, ffn1_up, ffn1_down, i)
        qt, k, vt, mc, ga = _mix_in(h.reshape(b, s, d), gains(norm_mix), w_in, conv_w, w_conv_out, cos, sin, i)
        o = _moba(qt, k, vt)
        h = _mix_ffn_ple(h, mc.reshape(t, d), ga.reshape(t, d), o.reshape(t, -1), p_rows, gains(norm_ffn2),
                         gains(norm_ple), norm_final.reshape(1, d), w_attn_out, w_o, ffn2_gate, ffn2_up,
                         ffn2_down, w_ple_gate, w_ple_proj, i, i == depth - 1)
    return h.reshape(b, s, d)
```

```python
import functools

import jax
import jax.numpy as jnp
from jax import lax
from jax.experimental import pallas as pl
from jax.experimental.pallas import tpu as pltpu

N_HEADS = 8
HEAD_DIM = 64
HALF_DIM = HEAD_DIM // 2
MOBA_BLOCK = 256
TOP_K = 3
MOBA_TILES = 4
ROPE_THETA = 10000.0
EPS = 1e-6

LANES = 128
SUBLANES = 8
HEADS_PER_GROUP = LANES // HEAD_DIM
BF16_SUBLANES = 2 * SUBLANES
V_ROWS = HEAD_DIM + BF16_SUBLANES
VMEM_LIMIT_BYTES = 56 * 1024 * 1024
MASKED = -1e30
LOG2_E = 1.4426950408889634

TOKEN_TILE = 512
STAGE_ROWS = 256
STAGE_DEPTH = 5
FF_CHUNK = 256
OUT_CHUNK = 256
PROJ_CHUNK = 512

F32 = jnp.float32
BF16 = jnp.bfloat16


def _dot(a, b):
    return jnp.dot(a, b, preferred_element_type=F32)


def _rmsnorm(x, gain):
    ms = jnp.mean(x * x, axis=-1, keepdims=True)
    return x * lax.rsqrt(ms + EPS) * gain


def _prenorm(x, gain):
    r = lax.rsqrt(jnp.mean(x * x, axis=-1, keepdims=True) + EPS)
    return (x * gain).astype(BF16), r


def _sigmoid(x):
    return 1.0 / (1.0 + jnp.exp(-x))


def _resident(shape):
    nd = len(shape)
    return pl.BlockSpec(shape, lambda *_: (0,) * nd, pipeline_mode=pl.Buffered(1))


def _layer(stacked_shape, layer):
    nd = len(stacked_shape) - 1
    return pl.BlockSpec((None,) + tuple(stacked_shape[1:]), lambda *_: (layer,) + (0,) * nd,
                        pipeline_mode=pl.Buffered(1))


def _bf16(ref, rows=slice(None), cols=slice(None)):
    return ref[rows, cols].astype(BF16)


def _params(*semantics):
    return pltpu.CompilerParams(dimension_semantics=semantics, vmem_limit_bytes=VMEM_LIMIT_BYTES)


def _ffn_rows(h_ref, gain_ref, wg_ref, wu_ref, wd_ref, o_ref, act_ref):
    x = h_ref[...]
    n, r = _prenorm(x, gain_ref[...])
    d_ff = wg_ref.shape[1]
    for c in range(d_ff // FF_CHUNK):
        cols = slice(c * FF_CHUNK, (c + 1) * FF_CHUNK)
        g = _dot(n, _bf16(wg_ref, cols=cols)) * r
        u = _dot(n, _bf16(wu_ref, cols=cols)) * r
        act_ref[:, cols] = (g * _sigmoid(g) * u).astype(BF16)
    for c in range(x.shape[1] // OUT_CHUNK):
        cols = slice(c * OUT_CHUNK, (c + 1) * OUT_CHUNK)
        o_ref[:, cols] = x[:, cols] + 0.5 * _dot(act_ref[...], _bf16(wd_ref, cols=cols))


def _ffn_rope_kernel(h_ref, gain_ref, wg_ref, wu_ref, wd_ref, pos_ref, inv_freq_ref, o_ref, cos_ref, sin_ref,
                     act_ref):
    ang = pos_ref[...].astype(F32) * inv_freq_ref[...]
    lane = lax.broadcasted_iota(jnp.int32, (1, LANES), 1)
    first_half = (lane % HEAD_DIM) < HALF_DIM
    cos_ref[...] = jnp.cos(ang)
    sin = jnp.sin(ang)
    sin_ref[...] = jnp.where(first_half, -sin, sin)
    _ffn_rows(h_ref, gain_ref, wg_ref, wu_ref, wd_ref, o_ref, act_ref)


def _ffn(h, gains, wg, wu, wd, layer, rope=None):
    t, d = h.shape
    d_ff = wg.shape[2]
    tile = min(TOKEN_TILE, t)

    def rows(width):
        return pl.BlockSpec((tile, width), lambda i: (i, 0))

    in_specs = [rows(d), _layer(gains.shape, layer), _layer(wg.shape, layer), _layer(wu.shape, layer),
                _layer(wd.shape, layer)]
    args = [h, gains, wg, wu, wd]
    out_specs, out_shape, body = rows(d), jax.ShapeDtypeStruct((t, d), F32), _ffn_rows
    if rope is not None:
        positions, inv_freq = rope
        in_specs += [rows(1), _resident(inv_freq.shape)]
        args += [positions, inv_freq]
        out_specs = [rows(d), rows(LANES), rows(LANES)]
        out_shape = [out_shape] + [jax.ShapeDtypeStruct((t, LANES), F32)] * 2
        body = _ffn_rope_kernel
    return pl.pallas_call(
        body,
        grid=(t // tile,),
        in_specs=in_specs,
        out_specs=out_specs,
        out_shape=out_shape,
        scratch_shapes=[pltpu.VMEM((tile, d_ff), BF16)],
        compiler_params=_params("parallel"),
        name="ffn" if rope is None else "ffn_rope",
    )(*args)


def _col_windows(src_hbm, layer, dst_ref, width):
    return [(src_hbm.at[layer, :, c:c + width], dst_ref.at[:, c:c + width])
            for c in range(0, dst_ref.shape[1], width)]


def _row_windows(src_hbm, layer, dst_ref, height):
    return [(src_hbm.at[layer, r:r + height, :], dst_ref.at[r:r + height, :])
            for r in range(0, dst_ref.shape[0], height)]


def _stage_params(jobs, ring_ref, sem_ref):
    depth = ring_ref.shape[0]

    def copy(i):
        return pltpu.make_async_copy(jobs[i][0], ring_ref.at[i % depth], sem_ref.at[i % depth])

    for i in range(min(depth - 1, len(jobs))):
        copy(i).start()
    for i, (_, dst) in enumerate(jobs):
        if i + depth - 1 < len(jobs):
            copy(i + depth - 1).start()
        copy(i).wait()
        dst[...] = ring_ref[i % depth].astype(BF16)


def _mix_ffn_ple_kernel(h_ref, mc_ref, ga_ref, attn_ref, p_ref, gain_ref, ple_gain_ref, final_gain_ref,
                        w_attn_out_hbm, w_o_hbm, wg_hbm, wu_hbm, wd_hbm, w_gate_hbm, w_proj_hbm, o_ref,
                        w_attn_out_ref, w_o_ref, wg_ref, wu_ref, wd_ref, w_gate_ref, w_proj_ref,
                        col_ring, row_ring, col_sem, row_sem, act_ref, x_ref, *, layer, final):
    @pl.when(pl.program_id(0) == 0)
    def _():
        row_jobs = (_row_windows(w_attn_out_hbm, layer, w_attn_out_ref, STAGE_ROWS)
                    + _row_windows(wd_hbm, layer, wd_ref, STAGE_ROWS)
                    + _row_windows(w_proj_hbm, layer, w_proj_ref, STAGE_ROWS))
        col_jobs = _col_windows(w_o_hbm, layer, w_o_ref, FF_CHUNK)
        for gate_job, up_job in zip(_col_windows(wg_hbm, layer, wg_ref, FF_CHUNK),
                                    _col_windows(wu_hbm, layer, wu_ref, FF_CHUNK)):
            col_jobs += [gate_job, up_job]
        col_jobs += _col_windows(w_gate_hbm, layer, w_gate_ref, FF_CHUNK)
        _stage_params(row_jobs, row_ring, row_sem)
        _stage_params(col_jobs, col_ring, col_sem)

    y_attn = _dot(attn_ref[...], w_attn_out_ref[...])
    merged = mc_ref[...].astype(F32) + ga_ref[...].astype(F32) * y_attn
    x_ref[...] = h_ref[...] + _dot(merged.astype(BF16), w_o_ref[...])
    _ffn_rows(x_ref, gain_ref, wg_ref, wu_ref, wd_ref, o_ref, act_ref)
    y = o_ref[...]
    n, r = _prenorm(y, ple_gain_ref[...])
    gate = _sigmoid(_dot(n, w_gate_ref[...]) * r)
    y = y + gate * _dot(p_ref[...].astype(BF16), w_proj_ref[...])
    o_ref[...] = _rmsnorm(y, final_gain_ref[...]) if final else y


def _mix_ffn_ple(h, mc, ga, attn, p, gains, ple_gains, final_gain, w_attn_out, w_o, wg, wu, wd, w_gate, w_proj,
                 layer, final):
    t, d = h.shape
    d_ff = wg.shape[2]
    aw = attn.shape[1]
    pd = p.shape[2]
    tile = min(TOKEN_TILE, t)

    def rows(width):
        return pl.BlockSpec((tile, width), lambda i: (i, 0))

    hbm = pl.BlockSpec(memory_space=pl.ANY)
    return pl.pallas_call(
        functools.partial(_mix_ffn_ple_kernel, layer=layer, final=final),
        grid=(t // tile,),
        in_specs=[rows(d), rows(d), rows(d), rows(aw), pl.BlockSpec((None, tile, pd), lambda i: (layer, i, 0)),
                  _layer(gains.shape, layer), _layer(ple_gains.shape, layer), _resident(final_gain.shape)]
                 + [hbm] * 7,
        out_specs=rows(d),
        out_shape=jax.ShapeDtypeStruct((t, d), F32),
        scratch_shapes=[
            pltpu.VMEM((aw, d), BF16), pltpu.VMEM((d, d), BF16), pltpu.VMEM((d, d_ff), BF16),
            pltpu.VMEM((d, d_ff), BF16), pltpu.VMEM((d_ff, d), BF16), pltpu.VMEM((d, d), BF16),
            pltpu.VMEM((pd, d), BF16),
            pltpu.VMEM((STAGE_DEPTH, d, FF_CHUNK), F32), pltpu.VMEM((STAGE_DEPTH, STAGE_ROWS, d), F32),
            pltpu.SemaphoreType.DMA((STAGE_DEPTH,)), pltpu.SemaphoreType.DMA((STAGE_DEPTH,)),
            pltpu.VMEM((tile, d_ff), BF16), pltpu.VMEM((tile, d), F32),
        ],
        compiler_params=_params("arbitrary"),
        name="mix_ffn_ple",
    )(h, mc, ga, attn, p, gains, ple_gains, final_gain, w_attn_out, w_o, wg, wu, wd, w_gate, w_proj)


def _mix_in_kernel(h_ref, gain_ref, w_in_ref, conv_w_ref, w_conv_out_ref, cos_ref, sin_ref,
                   qt_ref, k_ref, vt_ref, mc_ref, ga_ref, u_ref, *, conv_width, attn_width):
    tile = h_ref.shape[1]
    cw, aw = conv_width, attn_width
    d = h_ref.shape[2]
    @pl.when(pl.program_id(1) == 0)
    def _():
        u_ref[0:SUBLANES, :] = jnp.zeros((SUBLANES, cw), F32)

    n, r = _prenorm(h_ref[0], gain_ref[...])

    def proj(col0, width):
        return _dot(n, _bf16(w_in_ref, cols=slice(col0, col0 + width))) * r

    b_gate, c_gate, xc = proj(0, cw), proj(cw, cw), proj(2 * cw, cw)
    u = c_gate * xc
    u_ref[SUBLANES:SUBLANES + tile, :] = u

    cos = cos_ref[0]
    sin = sin_ref[0]
    lane = lax.broadcasted_iota(jnp.int32, (1, LANES), 1)
    first_half = (lane % HEAD_DIM) < HALF_DIM
    scale = HEAD_DIM ** -0.5 * LOG2_E
    dim_row = lax.broadcasted_iota(jnp.int32, (LANES, 1), 0)
    blocks = range(tile // MOBA_BLOCK)

    def rotary(xg):
        partner = jnp.where(first_half, pltpu.roll(xg, LANES - HALF_DIM, 1), pltpu.roll(xg, HALF_DIM, 1))
        return xg * cos + partner * sin

    q_rows, k_rows, v_rows = proj(3 * cw, aw), proj(3 * cw + aw, aw), proj(3 * cw + 2 * aw, aw)
    for g in range(aw // LANES):
        q_t = (rotary(q_rows[:, g * LANES:(g + 1) * LANES]) * scale).T
        for hh in range(HEADS_PER_GROUP):
            q_th = jnp.where((dim_row // HEAD_DIM) == hh, q_t, 0.0).astype(qt_ref.dtype)
            for j in blocks:
                qt_ref[0, j, g * HEADS_PER_GROUP + hh] = q_th[:, j * MOBA_BLOCK:(j + 1) * MOBA_BLOCK]
    for g in range(aw // LANES):
        cols = slice(g * LANES, (g + 1) * LANES)
        k_ref[0, :, cols] = rotary(k_rows[:, cols]).astype(k_ref.dtype)
    ones =jnp.ones((V_ROWS - HEAD_DIM, MOBA_BLOCK), vt_ref.dtype)
    for g in range(aw // LANES):
        v_t = v_rows[:, g * LANES:(g + 1) * LANES].T.astype(vt_ref.dtype)
        for hh in range(HEADS_PER_GROUP):
            for j in blocks:
                head = g * HEADS_PER_GROUP + hh
                vt_ref[0, j, head, 0:HEAD_DIM, :] = v_t[hh * HEAD_DIM:(hh + 1) * HEAD_DIM,
                                                        j * MOBA_BLOCK:(j + 1) * MOBA_BLOCK]
                vt_ref[0, j, head, HEAD_DIM:V_ROWS, :] = ones
    base = 3 * cw + 3 * aw
    for c in range(d // PROJ_CHUNK):
        cols = slice(c * PROJ_CHUNK, (c + 1) * PROJ_CHUNK)
        ga_ref[0, :, cols] = _sigmoid(proj(base + d + c * PROJ_CHUNK, PROJ_CHUNK)).astype(ga_ref.dtype)
    conv_gate = [_sigmoid(proj(base + c * PROJ_CHUNK, PROJ_CHUNK)) for c in range(d // PROJ_CHUNK)]

    u_prev1 = u_ref[SUBLANES - 1:SUBLANES - 1 + tile, :]
    u_prev2 = u_ref[SUBLANES - 2:SUBLANES - 2 + tile, :]
    conv_w = conv_w_ref[...]
    conv = u_prev2 * conv_w[0:1, :] + u_prev1 * conv_w[1:2, :] + u * conv_w[2:3, :]
    u_ref[0:SUBLANES, :] = u[tile - SUBLANES:tile, :]
    gated = (b_gate * conv).astype(BF16)
    for c in range(d // PROJ_CHUNK):
        cols = slice(c * PROJ_CHUNK, (c + 1) * PROJ_CHUNK)
        y_conv = _dot(gated, _bf16(w_conv_out_ref, cols=cols))
        mc_ref[0, :, cols] = (conv_gate[c] * y_conv).astype(mc_ref.dtype)


def _mix_in(h, gains, w_in, conv_w, w_conv_out, cos, sin, layer):
    b, s, d = h.shape
    cw = conv_w.shape[2]
    aw = N_HEADS * HEAD_DIM
    tile = min(TOKEN_TILE, s)

    n_blocks = s // MOBA_BLOCK

    def rows(width):
        return pl.BlockSpec((1, tile, width), lambda i, j: (i, j, 0))

    def per_block(height):
        return pl.BlockSpec((1, tile // MOBA_BLOCK, N_HEADS, height, MOBA_BLOCK), lambda i, j: (i, j, 0, 0, 0))

    return pl.pallas_call(
        functools.partial(_mix_in_kernel, conv_width=cw, attn_width=aw),
        grid=(b, s // tile),
        in_specs=[rows(d), _layer(gains.shape, layer), _layer(w_in.shape, layer), _layer(conv_w.shape, layer),
                  _layer(w_conv_out.shape, layer), rows(LANES), rows(LANES)],
        out_specs=[per_block(LANES), rows(aw), per_block(V_ROWS), rows(d), rows(d)],
        out_shape=[jax.ShapeDtypeStruct((b, n_blocks, N_HEADS, LANES, MOBA_BLOCK), BF16),
                   jax.ShapeDtypeStruct((b, s, aw), BF16),
                   jax.ShapeDtypeStruct((b, n_blocks, N_HEADS, V_ROWS, MOBA_BLOCK), BF16),
                   jax.ShapeDtypeStruct((b, s, d), BF16), jax.ShapeDtypeStruct((b, s, d), BF16)],
        scratch_shapes=[pltpu.VMEM((tile + SUBLANES, cw), F32)],
        compiler_params=_params("parallel", "arbitrary"),
        name="mix_in",
    )(h, gains, w_in, conv_w, w_conv_out, cos, sin)


def _moba_kernel(qt_ref, k_ref, vt_ref, o_ref, kmean_ref, selb_ref, s0_ref, stat0_ref, s1_ref, stat1_ref, m_ref,
                 l_ref, acc_ref, *, n_blocks):
    group = pl.program_id(1)
    blk = MOBA_BLOCK
    groups = N_HEADS // HEADS_PER_GROUP

    @pl.when(group == 0)
    def _():
        for n in range(n_blocks):
            rows = slice(n * blk, (n + 1) * blk)
            kmean_ref[n:n + 1, :] = jnp.mean(k_ref[0, rows, :].astype(F32), axis=0, keepdims=True)

    blk_row = lax.broadcasted_iota(jnp.int32, (n_blocks, 1), 0)
    blk_row_f = blk_row.astype(F32)

    def select_blocks(tile, qi):
        is_past = blk_row < qi
        for g in range(groups):
            kmean = kmean_ref[:, g * LANES:(g + 1) * LANES]
            kmean_hi = kmean.astype(BF16)
            kmean_lo = (kmean - kmean_hi.astype(F32)).astype(BF16)
            for hh in range(HEADS_PER_GROUP):
                h = g * HEADS_PER_GROUP + hh
                q_th = qt_ref[0, tile, h]
                gate = _dot(kmean_hi, q_th) + _dot(kmean_lo, q_th)
                gate = jnp.where(is_past, gate, -jnp.inf)
                keep = jnp.full((n_blocks, blk), MASKED, F32)
                for _ in range(TOP_K):
                    best = jnp.max(gate, axis=0, keepdims=True)
                    first = jnp.min(jnp.where(gate == best, blk_row_f, float(n_blocks)), axis=0, keepdims=True)
                    pick = blk_row_f == first
                    keep = jnp.where(pick, 0.0, keep)
                    gate = jnp.where(pick, -jnp.inf, gate)
                selb_ref[tile, h] = jnp.where(is_past, keep, MASKED)

    slots = ((s0_ref, stat0_ref), (s1_ref, stat1_ref))

    def score_stage(tile, h, n, slot, diag):
        s_ref, stat_ref = slots[slot]
        g = h // HEADS_PER_GROUP
        start = pl.multiple_of(n * blk, blk)
        s_t = _dot(k_ref[0, pl.ds(start, blk), g * LANES:(g + 1) * LANES], qt_ref[0, tile, h])
        if diag:
            key = lax.broadcasted_iota(jnp.int32, (blk, blk), 0)
            qry = lax.broadcasted_iota(jnp.int32, (blk, blk), 1)
            s_t = jnp.where(key <= qry, s_t, MASKED)
            keep = jnp.zeros((1, blk), F32)
        else:
            keep = selb_ref[tile, h, pl.ds(n, 1), :]
        m_old = m_ref[tile, h]
        m_new = jnp.maximum(m_old, jnp.max(s_t, axis=0, keepdims=True) + keep)
        s_ref[h] = s_t
        m_ref[tile, h] = m_new
        stat_ref[h, 0:1, :] = m_new
        stat_ref[h, 1:2, :] = jnp.exp2(m_old - m_new)
        stat_ref[h, 2:3, :] = keep

    def value_stage(tile, h, n, slot):
        s_ref, stat_ref = slots[slot]
        shift = stat_ref[h, 0:1, :]
        alpha = stat_ref[h, 1:2, :]
        keep = stat_ref[h, 2:3, :]
        p_t = jnp.exp2(s_ref[h] - shift).astype(BF16)
        pv_t = _dot(vt_ref[0, n, h], p_t)
        pv_t = jnp.where(keep == 0.0, pv_t, 0.0)
        acc_ref[tile, h] = alpha * acc_ref[tile, h] + pv_t[:HEAD_DIM]
        l_ref[tile, h] = alpha * l_ref[tile, h] + pv_t[HEAD_DIM:HEAD_DIM + 1]

    def reset(tile):
        for h in range(N_HEADS):
            m_ref[tile, h] = jnp.full((1, blk), MASKED, F32)
            l_ref[tile, h] = jnp.zeros((1, blk), F32)
            acc_ref[tile, h] = jnp.zeros((HEAD_DIM, blk), F32)

    def step(tile, qi, n, waiting):
        prev = jnp.where(n == 0, qi, n - 1)
        for h in range(N_HEADS):
            score_stage(tile, h, n, 1 - waiting, False)
            value_stage(tile, h, prev, waiting)

    def write_out(tile):
        for g in range(groups):
            heads = [g * HEADS_PER_GROUP + hh for hh in range(HEADS_PER_GROUP)]
            o_t = jnp.concatenate([acc_ref[tile, h] / l_ref[tile, h] for h in heads], axis=0)
            o_ref[0, tile * blk:(tile + 1) * blk, g * LANES:(g + 1) * LANES] = o_t.T.astype(o_ref.dtype)

    base = MOBA_TILES * group
    pending = None
    first_slot = 0
    for tile in range(MOBA_TILES):
        qi = base + tile
        reset(tile)
        select_blocks(tile, qi)
        for h in range(N_HEADS):
            if pending is not None:
                value_stage(pending[0], h, pending[1], pending[2])
            score_stage(tile, h, qi, first_slot, True)
        if pending is not None:
            write_out(pending[0])

        def full_trip(t, carry, tile=tile, qi=qi, first_slot=first_slot):
            for i in range(MOBA_TILES):
                step(tile, qi, MOBA_TILES * t + i, (first_slot + i) % 2)
            return carry

        lax.fori_loop(0, group, full_trip, 0)
        for i in range(tile):
            step(tile, qi, base + i, (first_slot + i) % 2)
        last_slot = (first_slot + tile) % 2
        pending = (tile, jnp.where(qi == 0, qi, qi - 1), last_slot)
        first_slot = 1 - last_slot
    for h in range(N_HEADS):
        value_stage(pending[0], h, pending[1], pending[2])
    write_out(pending[0])


def _moba(qt, k, vt):
    b, s, aw = k.shape
    n_blocks = s // MOBA_BLOCK
    blk = MOBA_BLOCK
    tiles = MOBA_TILES
    assert n_blocks % tiles == 0 and tiles % 2 == 0
    return pl.pallas_call(
        functools.partial(_moba_kernel, n_blocks=n_blocks),
        grid=(b, n_blocks // tiles),
        in_specs=[pl.BlockSpec((1, tiles) + qt.shape[2:], lambda i, j: (i, j, 0, 0, 0)),
                  pl.BlockSpec((1, s, aw), lambda i, j: (i, 0, 0)),
                  pl.BlockSpec((1,) + vt.shape[1:], lambda i, j: (i, 0, 0, 0, 0))],
        out_specs=pl.BlockSpec((1, tiles * blk, aw), lambda i, j: (i, j, 0)),
        out_shape=jax.ShapeDtypeStruct((b, s, aw), BF16),
        scratch_shapes=[
            pltpu.VMEM((n_blocks, aw), F32),
            pltpu.VMEM((tiles, N_HEADS, n_blocks, blk), F32),
            pltpu.VMEM((N_HEADS, blk, blk), F32),
            pltpu.VMEM((N_HEADS, SUBLANES, blk), F32),
            pltpu.VMEM((N_HEADS, blk, blk), F32),
            pltpu.VMEM((N_HEADS, SUBLANES, blk), F32),
            pltpu.VMEM((tiles, N_HEADS, 1, blk), F32),
            pltpu.VMEM((tiles, N_HEADS, 1, blk), F32),
            pltpu.VMEM((tiles, N_HEADS, HEAD_DIM, blk), F32),
        ],
        compiler_params=_params("parallel", "arbitrary"),
        name="moba",
    )(qt, k, vt)


def kernel(x, p, positions, w_in, conv_w, w_conv_out, w_attn_out, w_o, ffn1_gate, ffn1_up, ffn1_down,
           ffn2_gate, ffn2_up, ffn2_down, norm_ffn1, norm_mix, norm_ffn2, norm_ple, w_ple_gate, w_ple_proj,
           norm_final):
    b, s, d = x.shape
    depth = w_in.shape[0]
    t = b * s
    assert s % MOBA_BLOCK == 0 and s % TOKEN_TILE == 0

    freq = jnp.arange(HALF_DIM, dtype=F32) / HALF_DIM
    inv_freq = jnp.tile(ROPE_THETA ** (-freq), LANES // HALF_DIM).reshape(1, LANES)

    def gains(g):
        return g.reshape(depth, 1, d)

    p_rows = p.reshape(depth, t, p.shape[-1])
    h = x.reshape(t, d)
    for i in range(depth):
        if i == 0:
            h, cos, sin = _ffn(h, gains(norm_ffn1), ffn1_gate, ffn1_up, ffn1_down, i,
                               rope=(positions.reshape(t, 1), inv_freq))
            cos, sin = cos.reshape(b, s, LANES), sin.reshape(b, s, LANES)
        else:
            h = _ffn(h, gains(norm_ffn1), ffn1_gate, ffn1_up, ffn1_down, i)
        qt, k, vt, mc, ga = _mix_in(h.reshape(b, s, d), gains(norm_mix), w_in, conv_w, w_conv_out, cos, sin, i)
        o = _moba(qt, k, vt)
        h = _mix_ffn_ple(h, mc.reshape(t, d), ga.reshape(t, d), o.reshape(t, -1), p_rows, gains(norm_ffn2),
                         gains(norm_ple), norm_final.reshape(1, d), w_attn_out, w_o, ffn2_gate, ffn2_up,
                         ffn2_down, w_ple_gate, w_ple_proj, i, i == depth - 1)
    return h.reshape(b, s, d)
```

```python
import functools

import jax
import jax.numpy as jnp
from jax import lax
from jax.experimental import pallas as pl
from jax.experimental.pallas import tpu as pltpu

N_HEADS = 8
HEAD_DIM = 64
HALF_DIM = HEAD_DIM // 2
MOBA_BLOCK = 256
TOP_K = 3
MOBA_TILES = 4
ROPE_THETA = 10000.0
EPS = 1e-6

LANES = 128
SUBLANES = 8
HEADS_PER_GROUP = LANES // HEAD_DIM
BF16_SUBLANES = 2 * SUBLANES
V_ROWS = HEAD_DIM + BF16_SUBLANES
VMEM_LIMIT_BYTES = 56 * 1024 * 1024
MASKED = -1e30
LOG2_E = 1.4426950408889634

TOKEN_TILE = 512
STAGE_ROWS = 256
STAGE_DEPTH = 5
FF_CHUNK = 256
OUT_CHUNK = 256
PROJ_CHUNK = 512

F32 = jnp.float32
BF16 = jnp.bfloat16


def _dot(a, b):
    return jnp.dot(a, b, preferred_element_type=F32)


def _rmsnorm(x, gain):
    ms = jnp.mean(x * x, axis=-1, keepdims=True)
    return x * lax.rsqrt(ms + EPS) * gain


def _prenorm(x, gain):
    r = lax.rsqrt(jnp.mean(x * x, axis=-1, keepdims=True) + EPS)
    return (x * gain).astype(BF16), r


def _sigmoid(x):
    return 1.0 / (1.0 + jnp.exp(-x))


def _resident(shape):
    nd = len(shape)
    return pl.BlockSpec(shape, lambda *_: (0,) * nd, pipeline_mode=pl.Buffered(1))


def _layer(stacked_shape, layer):
    nd = len(stacked_shape) - 1
    return pl.BlockSpec((None,) + tuple(stacked_shape[1:]), lambda *_: (layer,) + (0,) * nd,
                        pipeline_mode=pl.Buffered(1))


def _bf16(ref, rows=slice(None), cols=slice(None)):
    return ref[rows, cols].astype(BF16)


def _params(*semantics):
    return pltpu.CompilerParams(dimension_semantics=semantics, vmem_limit_bytes=VMEM_LIMIT_BYTES)


def _ffn_rows(h_ref, gain_ref, wg_ref, wu_ref, wd_ref, o_ref, act_ref):
    x = h_ref[...]
    n, r = _prenorm(x, gain_ref[...])
    d_ff = wg_ref.shape[1]
    for c in range(d_ff // FF_CHUNK):
        cols = slice(c * FF_CHUNK, (c + 1) * FF_CHUNK)
        g = _dot(n, _bf16(wg_ref, cols=cols)) * r
        u = _dot(n, _bf16(wu_ref, cols=cols)) * r
        act_ref[:, cols] = (g * _sigmoid(g) * u).astype(BF16)
    for c in range(x.shape[1] // OUT_CHUNK):
        cols = slice(c * OUT_CHUNK, (c + 1) * OUT_CHUNK)
        o_ref[:, cols] = x[:, cols] + 0.5 * _dot(act_ref[...], _bf16(wd_ref, cols=cols))


def _ffn_rope_kernel(h_ref, gain_ref, wg_ref, wu_ref, wd_ref, pos_ref, inv_freq_ref, o_ref, cos_ref, sin_ref,
                     act_ref):
    ang = pos_ref[...].astype(F32) * inv_freq_ref[...]
    lane = lax.broadcasted_iota(jnp.int32, (1, LANES), 1)
    first_half = (lane % HEAD_DIM) < HALF_DIM
    cos_ref[...] = jnp.cos(ang)
    sin = jnp.sin(ang)
    sin_ref[...] = jnp.where(first_half, -sin, sin)
    _ffn_rows(h_ref, gain_ref, wg_ref, wu_ref, wd_ref, o_ref, act_ref)


def _ffn(h, gains, wg, wu, wd, layer, rope=None):
    t, d = h.shape
    d_ff = wg.shape[2]
    tile = min(TOKEN_TILE, t)

    def rows(width):
        return pl.BlockSpec((tile, width), lambda i: (i, 0))

    in_specs = [rows(d), _layer(gains.shape, layer), _layer(wg.shape, layer), _layer(wu.shape, layer),
                _layer(wd.shape, layer)]
    args = [h, gains, wg, wu, wd]
    out_specs, out_shape, body = rows(d), jax.ShapeDtypeStruct((t, d), F32), _ffn_rows
    if rope is not None:
        positions, inv_freq = rope
        in_specs += [rows(1), _resident(inv_freq.shape)]
        args += [positions, inv_freq]
        out_specs = [rows(d), rows(LANES), rows(LANES)]
        out_shape = [out_shape] + [jax.ShapeDtypeStruct((t, LANES), F32)] * 2
        body = _ffn_rope_kernel
    return pl.pallas_call(
        body,
        grid=(t // tile,),
        in_specs=in_specs,
        out_specs=out_specs,
        out_shape=out_shape,
        scratch_shapes=[pltpu.VMEM((tile, d_ff), BF16)],
        compiler_params=_params("parallel"),
        name="ffn" if rope is None else "ffn_rope",
    )(*args)


def _col_windows(src_hbm, layer, dst_ref, width):
    return [(src_hbm.at[layer, :, c:c + width], dst_ref.at[:, c:c + width])
            for c in range(0, dst_ref.shape[1], width)]


def _row_windows(src_hbm, layer, dst_ref, height):
    return [(src_hbm.at[layer, r:r + height, :], dst_ref.at[r:r + height, :])
            for r in range(0, dst_ref.shape[0], height)]


def _stage_params(jobs, ring_ref, sem_ref):
    depth = ring_ref.shape[0]

    def copy(i):
        return pltpu.make_async_copy(jobs[i][0], ring_ref.at[i % depth], sem_ref.at[i % depth])

    for i in range(min(depth - 1, len(jobs))):
        copy(i).start()
    for i, (_, dst) in enumerate(jobs):
        if i + depth - 1 < len(jobs):
            copy(i + depth - 1).start()
        copy(i).wait()
        dst[...] = ring_ref[i % depth].astype(BF16)


def _mix_ffn_ple_kernel(h_ref, mc_ref, ga_ref, attn_ref, p_ref, gain_ref, ple_gain_ref, final_gain_ref,
                        w_attn_out_hbm, w_o_hbm, wg_hbm, wu_hbm, wd_hbm, w_gate_hbm, w_proj_hbm, o_ref,
                        w_attn_out_ref, w_o_ref, wg_ref, wu_ref, wd_ref, w_gate_ref, w_proj_ref,
                        col_ring, row_ring, col_sem, row_sem, act_ref, x_ref, *, layer, final):
    @pl.when(pl.program_id(0) == 0)
    def _():
        row_jobs = (_row_windows(w_attn_out_hbm, layer, w_attn_out_ref, STAGE_ROWS)
                    + _row_windows(wd_hbm, layer, wd_ref, STAGE_ROWS)
                    + _row_windows(w_proj_hbm, layer, w_proj_ref, STAGE_ROWS))
        col_jobs = _col_windows(w_o_hbm, layer, w_o_ref, FF_CHUNK)
        for gate_job, up_job in zip(_col_windows(wg_hbm, layer, wg_ref, FF_CHUNK),
                                    _col_windows(wu_hbm, layer, wu_ref, FF_CHUNK)):
            col_jobs += [gate_job, up_job]
        col_jobs += _col_windows(w_gate_hbm, layer, w_gate_ref, FF_CHUNK)
        _stage_params(row_jobs, row_ring, row_sem)
        _stage_params(col_jobs, col_ring, col_sem)

    y_attn = _dot(attn_ref[...], w_attn_out_ref[...])
    merged = mc_ref[...].astype(F32) + ga_ref[...].astype(F32) * y_attn
    x_ref[...] = h_ref[...] + _dot(merged.astype(BF16), w_o_ref[...])
    _ffn_rows(x_ref, gain_ref, wg_ref, wu_ref, wd_ref, o_ref, act_ref)
    y = o_ref[...]
    n, r = _prenorm(y, ple_gain_ref[...])
    gate = _sigmoid(_dot(n, w_gate_ref[...]) * r)
    y = y + gate * _dot(p_ref[...].astype(BF16), w_proj_ref[...])
    o_ref[...] = _rmsnorm(y, final_gain_ref[...]) if final else y


def _mix_ffn_ple(h, mc, ga, attn, p, gains, ple_gains, final_gain, w_attn_out, w_o, wg, wu, wd, w_gate, w_proj,
                 layer, final):
    t, d = h.shape
    d_ff = wg.shape[2]
    aw = attn.shape[1]
    pd = p.shape[2]
    tile = min(TOKEN_TILE, t)

    def rows(width):
        return pl.BlockSpec((tile, width), lambda i: (i, 0))

    hbm = pl.BlockSpec(memory_space=pl.ANY)
    return pl.pallas_call(
        functools.partial(_mix_ffn_ple_kernel, layer=layer, final=final),
        grid=(t // tile,),
        in_specs=[rows(d), rows(d), rows(d), rows(aw), pl.BlockSpec((None, tile, pd), lambda i: (layer, i, 0)),
                  _layer(gains.shape, layer), _layer(ple_gains.shape, layer), _resident(final_gain.shape)]
                 + [hbm] * 7,
        out_specs=rows(d),
        out_shape=jax.ShapeDtypeStruct((t, d), F32),
        scratch_shapes=[
            pltpu.VMEM((aw, d), BF16), pltpu.VMEM((d, d), BF16), pltpu.VMEM((d, d_ff), BF16),
            pltpu.VMEM((d, d_ff), BF16), pltpu.VMEM((d_ff, d), BF16), pltpu.VMEM((d, d), BF16),
            pltpu.VMEM((pd, d), BF16),
            pltpu.VMEM((STAGE_DEPTH, d, FF_CHUNK), F32), pltpu.VMEM((STAGE_DEPTH, STAGE_ROWS, d), F32),
            pltpu.SemaphoreType.DMA((STAGE_DEPTH,)), pltpu.SemaphoreType.DMA((STAGE_DEPTH,)),
            pltpu.VMEM((tile, d_ff), BF16), pltpu.VMEM((tile, d), F32),
        ],
        compiler_params=_params("arbitrary"),
        name="mix_ffn_ple",
    )(h, mc, ga, attn, p, gains, ple_gains, final_gain, w_attn_out, w_o, wg, wu, wd, w_gate, w_proj)


def _mix_in_kernel(h_ref, gain_ref, w_in_ref, conv_w_ref, w_conv_out_ref, cos_ref, sin_ref,
                   qt_ref, k_ref, vt_ref, mc_ref, ga_ref, u_ref, *, conv_width, attn_width):
    tile = h_ref.shape[1]
    cw, aw = conv_width, attn_width
    d = h_ref.shape[2]
    @pl.when(pl.program_id(1) == 0)
    def _():
        u_ref[0:SUBLANES, :] = jnp.zeros((SUBLANES, cw), F32)

    n, r = _prenorm(h_ref[0], gain_ref[...])

    def proj(col0, width):
        return _dot(n, _bf16(w_in_ref, cols=slice(col0, col0 + width))) * r

    b_gate, c_gate, xc = proj(0, cw), proj(cw, cw), proj(2 * cw, cw)
    u = c_gate * xc
    u_ref[SUBLANES:SUBLANES + tile, :] = u

    cos = cos_ref[0]
    sin = sin_ref[0]
    lane = lax.broadcasted_iota(jnp.int32, (1, LANES), 1)
    first_half = (lane % HEAD_DIM) < HALF_DIM
    scale = HEAD_DIM ** -0.5 * LOG2_E
    blocks = range(tile // MOBA_BLOCK)

    def rotary(xg):
        partner = jnp.where(first_half, pltpu.roll(xg, LANES - HALF_DIM, 1), pltpu.roll(xg, HALF_DIM, 1))
        return xg * cos + partner * sin

    q_rows, k_rows, v_rows = proj(3 * cw, aw), proj(3 * cw + aw, aw), proj(3 * cw + 2 * aw, aw)
    for g in range(aw // LANES):
        q_t = (rotary(q_rows[:, g * LANES:(g + 1) * LANES]) * scale).T.astype(qt_ref.dtype)
        for hh in range(HEADS_PER_GROUP):
            for j in blocks:
                toks = slice(j * MOBA_BLOCK, (j + 1) * MOBA_BLOCK)
                for other in range(HEADS_PER_GROUP):
                    dims = slice(other * HEAD_DIM, (other + 1) * HEAD_DIM)
                    qt_ref[0, j, g * HEADS_PER_GROUP + hh, dims, :] = (
                        q_t[dims, toks] if other == hh else jnp.zeros((HEAD_DIM, MOBA_BLOCK), qt_ref.dtype))
    for g in range(aw // LANES):
        cols = slice(g * LANES, (g + 1) * LANES)
        k_ref[0, g] = rotary(k_rows[:, cols]).astype(k_ref.dtype)
    ones =jnp.ones((V_ROWS - HEAD_DIM, MOBA_BLOCK), vt_ref.dtype)
    for g in range(aw // LANES):
        v_t = v_rows[:, g * LANES:(g + 1) * LANES].T.astype(vt_ref.dtype)
        for hh in range(HEADS_PER_GROUP):
            for j in blocks:
                head = g * HEADS_PER_GROUP + hh
                vt_ref[0, j, head, 0:HEAD_DIM, :] = v_t[hh * HEAD_DIM:(hh + 1) * HEAD_DIM,
                                                        j * MOBA_BLOCK:(j + 1) * MOBA_BLOCK]
                vt_ref[0, j, head, HEAD_DIM:V_ROWS, :] = ones
    base = 3 * cw + 3 * aw
    for c in range(d // PROJ_CHUNK):
        cols = slice(c * PROJ_CHUNK, (c + 1) * PROJ_CHUNK)
        ga_ref[0, :, cols] = _sigmoid(proj(base + d + c * PROJ_CHUNK, PROJ_CHUNK)).astype(ga_ref.dtype)
    conv_gate = [_sigmoid(proj(base + c * PROJ_CHUNK, PROJ_CHUNK)) for c in range(d // PROJ_CHUNK)]

    u_prev1 = u_ref[SUBLANES - 1:SUBLANES - 1 + tile, :]
    u_prev2 = u_ref[SUBLANES - 2:SUBLANES - 2 + tile, :]
    conv_w = conv_w_ref[...]
    conv = u_prev2 * conv_w[0:1, :] + u_prev1 * conv_w[1:2, :] + u * conv_w[2:3, :]
    u_ref[0:SUBLANES, :] = u[tile - SUBLANES:tile, :]
    gated = (b_gate * conv).astype(BF16)
    for c in range(d // PROJ_CHUNK):
        cols = slice(c * PROJ_CHUNK, (c + 1) * PROJ_CHUNK)
        y_conv = _dot(gated, _bf16(w_conv_out_ref, cols=cols))
        mc_ref[0, :, cols] = (conv_gate[c] * y_conv).astype(mc_ref.dtype)


def _mix_in(h, gains, w_in, conv_w, w_conv_out, cos, sin, layer):
    b, s, d = h.shape
    cw = conv_w.shape[2]
    aw = N_HEADS * HEAD_DIM
    tile = min(TOKEN_TILE, s)

    n_blocks = s // MOBA_BLOCK

    def rows(width):
        return pl.BlockSpec((1, tile, width), lambda i, j: (i, j, 0))

    def per_block(height):
        return pl.BlockSpec((1, tile // MOBA_BLOCK, N_HEADS, height, MOBA_BLOCK), lambda i, j: (i, j, 0, 0, 0))

    return pl.pallas_call(
        functools.partial(_mix_in_kernel, conv_width=cw, attn_width=aw),
        grid=(b, s // tile),
        in_specs=[rows(d), _layer(gains.shape, layer), _layer(w_in.shape, layer), _layer(conv_w.shape, layer),
                  _layer(w_conv_out.shape, layer), rows(LANES), rows(LANES)],
        out_specs=[per_block(LANES), pl.BlockSpec((1, aw // LANES, tile, LANES), lambda i, j: (i, 0, j, 0)),
                   per_block(V_ROWS), rows(d), rows(d)],
        out_shape=[jax.ShapeDtypeStruct((b, n_blocks, N_HEADS, LANES, MOBA_BLOCK), BF16),
                   jax.ShapeDtypeStruct((b, aw // LANES, s, LANES), BF16),
                   jax.ShapeDtypeStruct((b, n_blocks, N_HEADS, V_ROWS, MOBA_BLOCK), BF16),
                   jax.ShapeDtypeStruct((b, s, d), BF16), jax.ShapeDtypeStruct((b, s, d), BF16)],
        scratch_shapes=[pltpu.VMEM((tile + SUBLANES, cw), F32)],
        compiler_params=_params("parallel", "arbitrary"),
        name="mix_in",
    )(h, gains, w_in, conv_w, w_conv_out, cos, sin)


def _moba_kernel(qt_ref, k_ref, vt_ref, o_ref, kmean_ref, selb_ref, s0_ref, stat0_ref, s1_ref, stat1_ref, m_ref,
                 l_ref, acc_ref, *, n_blocks):
    group = pl.program_id(1)
    blk = MOBA_BLOCK
    groups = N_HEADS // HEADS_PER_GROUP

    @pl.when(group == 0)
    def _():
        for n in range(n_blocks):
            rows = slice(n * blk, (n + 1) * blk)
            for g in range(groups):
                kmean_ref[n:n + 1, g * LANES:(g + 1) * LANES] = jnp.mean(k_ref[0, g, rows, :].astype(F32), axis=0,
                                                                          keepdims=True)

    blk_row = lax.broadcasted_iota(jnp.int32, (n_blocks, 1), 0)
    blk_row_f = blk_row.astype(F32)

    def select_blocks(tile, qi):
        is_past = blk_row < qi
        for g in range(groups):
            kmean = kmean_ref[:, g * LANES:(g + 1) * LANES]
            kmean_hi = kmean.astype(BF16)
            kmean_lo = (kmean - kmean_hi.astype(F32)).astype(BF16)
            for hh in range(HEADS_PER_GROUP):
                h = g * HEADS_PER_GROUP + hh
                q_th = qt_ref[0, tile, h]
                gate = _dot(kmean_hi, q_th) + _dot(kmean_lo, q_th)
                gate = jnp.where(is_past, gate, -jnp.inf)
                keep = jnp.full((n_blocks, blk), MASKED, F32)
                for _ in range(TOP_K):
                    best = jnp.max(gate, axis=0, keepdims=True)
                    first = jnp.min(jnp.where(gate == best, blk_row_f, float(n_blocks)), axis=0, keepdims=True)
                    pick = blk_row_f == first
                    keep = jnp.where(pick, 0.0, keep)
                    gate = jnp.where(pick, -jnp.inf, gate)
                selb_ref[tile, h] = jnp.where(is_past, keep, MASKED)

    slots = ((s0_ref, stat0_ref), (s1_ref, stat1_ref))

    def score_stage(tile, h, n, slot, diag):
        s_ref, stat_ref = slots[slot]
        g = h // HEADS_PER_GROUP
        start = pl.multiple_of(n * blk, blk)
        s_t = _dot(k_ref[0, g, pl.ds(start, blk), :], qt_ref[0, tile, h])
        if diag:
            key = lax.broadcasted_iota(jnp.int32, (blk, blk), 0)
            qry = lax.broadcasted_iota(jnp.int32, (blk, blk), 1)
            s_t = jnp.where(key <= qry, s_t, MASKED)
            keep = jnp.zeros((1, blk), F32)
        else:
            keep = selb_ref[tile, h, pl.ds(n, 1), :]
        m_old = m_ref[tile, h]
        m_new = jnp.maximum(m_old, jnp.max(s_t, axis=0, keepdims=True) + keep)
        s_ref[h] = s_t
        m_ref[tile, h] = m_new
        stat_ref[h, 0:1, :] = m_new
        stat_ref[h, 1:2, :] = jnp.exp2(m_old - m_new)
        stat_ref[h, 2:3, :] = keep

    def value_stage(tile, h, n, slot):
        s_ref, stat_ref = slots[slot]
        shift = stat_ref[h, 0:1, :]
        alpha = stat_ref[h, 1:2, :]
        keep = stat_ref[h, 2:3, :]
        p_t = jnp.exp2(s_ref[h] - shift).astype(BF16)
        pv_t = _dot(vt_ref[0, n, h], p_t)
        pv_t = jnp.where(keep == 0.0, pv_t, 0.0)
        acc_ref[tile, h] = alpha * acc_ref[tile, h] + pv_t[:HEAD_DIM]
        l_ref[tile, h] = alpha * l_ref[tile, h] + pv_t[HEAD_DIM:HEAD_DIM + 1]

    def reset(tile):
        for h in range(N_HEADS):
            m_ref[tile, h] = jnp.full((1, blk), MASKED, F32)
            l_ref[tile, h] = jnp.zeros((1, blk), F32)
            acc_ref[tile, h] = jnp.zeros((HEAD_DIM, blk), F32)

    def step(tile, qi, n, waiting):
        prev = jnp.where(n == 0, qi, n - 1)
        for h in range(N_HEADS):
            score_stage(tile, h, n, 1 - waiting, False)
            value_stage(tile, h, prev, waiting)

    def write_out(tile):
        for g in range(groups):
            heads = [g * HEADS_PER_GROUP + hh for hh in range(HEADS_PER_GROUP)]
            o_t = jnp.concatenate([acc_ref[tile, h] / l_ref[tile, h] for h in heads], axis=0)
            o_ref[0, tile * blk:(tile + 1) * blk, g * LANES:(g + 1) * LANES] = o_t.T.astype(o_ref.dtype)

    base = MOBA_TILES * group
    pending = None
    first_slot = 0
    for tile in range(MOBA_TILES):
        qi = base + tile
        reset(tile)
        select_blocks(tile, qi)
        for h in range(N_HEADS):
            if pending is not None:
                value_stage(pending[0], h, pending[1], pending[2])
            score_stage(tile, h, qi, first_slot, True)
        if pending is not None:
            write_out(pending[0])

        def full_trip(t, carry, tile=tile, qi=qi, first_slot=first_slot):
            for i in range(MOBA_TILES):
                step(tile, qi, MOBA_TILES * t + i, (first_slot + i) % 2)
            return carry

        lax.fori_loop(0, group, full_trip, 0)
        for i in range(tile):
            step(tile, qi, base + i, (first_slot + i) % 2)
        last_slot = (first_slot + tile) % 2
        pending = (tile, jnp.where(qi == 0, qi, qi - 1), last_slot)
        first_slot = 1 - last_slot
    for h in range(N_HEADS):
        value_stage(pending[0], h, pending[1], pending[2])
    write_out(pending[0])


def _moba(qt, k, vt):
    b, groups, s, _ = k.shape
    aw = groups * LANES
    n_blocks = s // MOBA_BLOCK
    blk = MOBA_BLOCK
    tiles = MOBA_TILES
    assert n_blocks % tiles == 0 and tiles % 2 == 0
    return pl.pallas_call(
        functools.partial(_moba_kernel, n_blocks=n_blocks),
        grid=(b, n_blocks // tiles),
        in_specs=[pl.BlockSpec((1, tiles) + qt.shape[2:], lambda i, j: (i, j, 0, 0, 0)),
                  pl.BlockSpec((1,) + k.shape[1:], lambda i, j: (i, 0, 0, 0)),
                  pl.BlockSpec((1,) + vt.shape[1:], lambda i, j: (i, 0, 0, 0, 0))],
        out_specs=pl.BlockSpec((1, tiles * blk, aw), lambda i, j: (i, j, 0)),
        out_shape=jax.ShapeDtypeStruct((b, s, aw), BF16),
        scratch_shapes=[
            pltpu.VMEM((n_blocks, aw), F32),
            pltpu.VMEM((tiles, N_HEADS, n_blocks, blk), F32),
            pltpu.VMEM((N_HEADS, blk, blk), F32),
            pltpu.VMEM((N_HEADS, SUBLANES, blk), F32),
            pltpu.VMEM((N_HEADS, blk, blk), F32),
            pltpu.VMEM((N_HEADS, SUBLANES, blk), F32),
            pltpu.VMEM((tiles, N_HEADS, 1, blk), F32),
            pltpu.VMEM((tiles, N_HEADS, 1, blk), F32),
            pltpu.VMEM((tiles, N_HEADS, HEAD_DIM, blk), F32),
        ],
        compiler_params=_params("parallel", "arbitrary"),
        name="moba",
    )(qt, k, vt)


def kernel(x, p, positions, w_in, conv_w, w_conv_out, w_attn_out, w_o, ffn1_gate, ffn1_up, ffn1_down,
           ffn2_gate, ffn2_up, ffn2_down, norm_ffn1, norm_mix, norm_ffn2, norm_ple, w_ple_gate, w_ple_proj,
           norm_final):
    b, s, d = x.shape
    depth = w_in.shape[0]
    t = b * s
    assert s % MOBA_BLOCK == 0 and s % TOKEN_TILE == 0

    freq = jnp.arange(HALF_DIM, dtype=F32) / HALF_DIM
    inv_freq = jnp.tile(ROPE_THETA ** (-freq), LANES // HALF_DIM).reshape(1, LANES)

    def gains(g):
        return g.reshape(depth, 1, d)

    p_rows = p.reshape(depth, t, p.shape[-1])
    h = x.reshape(t, d)
    for i in range(depth):
        if i == 0:
            h, cos, sin = _ffn(h, gains(norm_ffn1), ffn1_gate, ffn1_up, ffn1_down, i,
                               rope=(positions.reshape(t, 1), inv_freq))
            cos, sin = cos.reshape(b, s, LANES), sin.reshape(b, s, LANES)
        else:
            h = _ffn(h, gains(norm_ffn1), ffn1_gate, ffn1_up, ffn1_down, i)
        qt, k, vt, mc, ga = _mix_in(h.reshape(b, s, d), gains(norm_mix), w_in, conv_w, w_conv_out, cos, sin, i)
        o = _moba(qt, k, vt)
        h = _mix_ffn_ple(h, mc.reshape(t, d), ga.reshape(t, d), o.reshape(t, -1), p_rows, gains(norm_ffn2),
                         gains(norm_ple), norm_final.reshape(1, d), w_attn_out, w_o, ffn2_gate, ffn2_up,
                         ffn2_down, w_ple_gate, w_ple_proj, i, i == depth - 1)
    return h.reshape(b, s, d)
```

```python
import functools

import jax
import jax.numpy as jnp
from jax import lax
from jax.experimental import pallas as pl
from jax.experimental.pallas import tpu as pltpu

N_HEADS = 8
HEAD_DIM = 64
HALF_DIM = HEAD_DIM // 2
MOBA_BLOCK = 256
TOP_K = 3
MOBA_TILES = 4
ROPE_THETA = 10000.0
EPS = 1e-6

LANES = 128
SUBLANES = 8
HEADS_PER_GROUP = LANES // HEAD_DIM
BF16_SUBLANES = 2 * SUBLANES
V_ROWS = HEAD_DIM + BF16_SUBLANES
VMEM_LIMIT_BYTES = 56 * 1024 * 1024
MASKED = -1e30
LOG2_E = 1.4426950408889634

TOKEN_TILE = 512
STAGE_ROWS = 256
STAGE_DEPTH = 5
FF_CHUNK = 256
OUT_CHUNK = 256
PROJ_CHUNK = 512

F32 = jnp.float32
BF16 = jnp.bfloat16


def _dot(a, b):
    return jnp.dot(a, b, preferred_element_type=F32)


def _rmsnorm(x, gain):
    ms = jnp.mean(x * x, axis=-1, keepdims=True)
    return x * lax.rsqrt(ms + EPS) * gain


def _prenorm(x, gain):
    r = lax.rsqrt(jnp.mean(x * x, axis=-1, keepdims=True) + EPS)
    return (x * gain).astype(BF16), r


def _sigmoid(x):
    return 1.0 / (1.0 + jnp.exp(-x))


def _resident(shape):
    nd = len(shape)
    return pl.BlockSpec(shape, lambda *_: (0,) * nd, pipeline_mode=pl.Buffered(1))


def _layer(stacked_shape, layer):
    nd = len(stacked_shape) - 1
    return pl.BlockSpec((None,) + tuple(stacked_shape[1:]), lambda *_: (layer,) + (0,) * nd,
                        pipeline_mode=pl.Buffered(1))


def _bf16(ref, rows=slice(None), cols=slice(None)):
    return ref[rows, cols].astype(BF16)


def _params(*semantics):
    return pltpu.CompilerParams(dimension_semantics=semantics, vmem_limit_bytes=VMEM_LIMIT_BYTES)


def _ffn_rows(h_ref, gain_ref, wg_ref, wu_ref, wd_ref, o_ref, act_ref):
    x = h_ref[...]
    n, r = _prenorm(x, gain_ref[...])
    d_ff = wg_ref.shape[1]
    for c in range(d_ff // FF_CHUNK):
        cols = slice(c * FF_CHUNK, (c + 1) * FF_CHUNK)
        g = _dot(n, _bf16(wg_ref, cols=cols)) * r
        u = _dot(n, _bf16(wu_ref, cols=cols)) * r
        act_ref[:, cols] = (g * _sigmoid(g) * u).astype(BF16)
    for c in range(x.shape[1] // OUT_CHUNK):
        cols = slice(c * OUT_CHUNK, (c + 1) * OUT_CHUNK)
        o_ref[:, cols] = x[:, cols] + 0.5 * _dot(act_ref[...], _bf16(wd_ref, cols=cols))


def _ffn_rope_kernel(h_ref, gain_ref, wg_ref, wu_ref, wd_ref, pos_ref, inv_freq_ref, o_ref, cos_ref, sin_ref,
                     act_ref):
    ang = pos_ref[...].astype(F32) * inv_freq_ref[...]
    lane = lax.broadcasted_iota(jnp.int32, (1, LANES), 1)
    first_half = (lane % HEAD_DIM) < HALF_DIM
    cos_ref[...] = jnp.cos(ang)
    sin = jnp.sin(ang)
    sin_ref[...] = jnp.where(first_half, -sin, sin)
    _ffn_rows(h_ref, gain_ref, wg_ref, wu_ref, wd_ref, o_ref, act_ref)


def _ffn(h, gains, wg, wu, wd, layer, rope=None):
    t, d = h.shape
    d_ff = wg.shape[2]
    tile = min(TOKEN_TILE, t)

    def rows(width):
        return pl.BlockSpec((tile, width), lambda i: (i, 0))

    in_specs = [rows(d), _layer(gains.shape, layer), _layer(wg.shape, layer), _layer(wu.shape, layer),
                _layer(wd.shape, layer)]
    args = [h, gains, wg, wu, wd]
    out_specs, out_shape, body = rows(d), jax.ShapeDtypeStruct((t, d), F32), _ffn_rows
    if rope is not None:
        positions, inv_freq = rope
        in_specs += [rows(1), _resident(inv_freq.shape)]
        args += [positions, inv_freq]
        out_specs = [rows(d), rows(LANES), rows(LANES)]
        out_shape = [out_shape] + [jax.ShapeDtypeStruct((t, LANES), F32)] * 2
        body = _ffn_rope_kernel
    return pl.pallas_call(
        body,
        grid=(t // tile,),
        in_specs=in_specs,
        out_specs=out_specs,
        out_shape=out_shape,
        scratch_shapes=[pltpu.VMEM((tile, d_ff), BF16)],
        compiler_params=_params("parallel"),
        name="ffn" if rope is None else "ffn_rope",
    )(*args)


def _col_windows(src_hbm, layer, dst_ref, width):
    return [(src_hbm.at[layer, :, c:c + width], dst_ref.at[:, c:c + width])
            for c in range(0, dst_ref.shape[1], width)]


def _row_windows(src_hbm, layer, dst_ref, height):
    return [(src_hbm.at[layer, r:r + height, :], dst_ref.at[r:r + height, :])
            for r in range(0, dst_ref.shape[0], height)]


def _stage_params(jobs, ring_ref, sem_ref):
    depth = ring_ref.shape[0]

    def copy(i):
        return pltpu.make_async_copy(jobs[i][0], ring_ref.at[i % depth], sem_ref.at[i % depth])

    for i in range(min(depth - 1, len(jobs))):
        copy(i).start()
    for i, (_, dst) in enumerate(jobs):
        if i + depth - 1 < len(jobs):
            copy(i + depth - 1).start()
        copy(i).wait()
        dst[...] = ring_ref[i % depth].astype(BF16)


def _mix_ffn_ple_kernel(h_ref, mc_ref, ga_ref, attn_ref, p_ref, gain_ref, ple_gain_ref, final_gain_ref,
                        w_attn_out_hbm, w_o_hbm, wg_hbm, wu_hbm, wd_hbm, w_gate_hbm, w_proj_hbm, o_ref,
                        w_attn_out_ref, w_o_ref, wg_ref, wu_ref, wd_ref, w_gate_ref, w_proj_ref,
                        col_ring, row_ring, col_sem, row_sem, act_ref, x_ref, *, layer, final):
    @pl.when(pl.program_id(0) == 0)
    def _():
        row_jobs = (_row_windows(w_attn_out_hbm, layer, w_attn_out_ref, STAGE_ROWS)
                    + _row_windows(wd_hbm, layer, wd_ref, STAGE_ROWS)
                    + _row_windows(w_proj_hbm, layer, w_proj_ref, STAGE_ROWS))
        col_jobs = _col_windows(w_o_hbm, layer, w_o_ref, FF_CHUNK)
        for gate_job, up_job in zip(_col_windows(wg_hbm, layer, wg_ref, FF_CHUNK),
                                    _col_windows(wu_hbm, layer, wu_ref, FF_CHUNK)):
            col_jobs += [gate_job, up_job]
        col_jobs += _col_windows(w_gate_hbm, layer, w_gate_ref, FF_CHUNK)
        _stage_params(row_jobs, row_ring, row_sem)
        _stage_params(col_jobs, col_ring, col_sem)

    y_attn = _dot(attn_ref[...], w_attn_out_ref[...])
    merged = mc_ref[...].astype(F32) + ga_ref[...].astype(F32) * y_attn
    x_ref[...] = h_ref[...] + _dot(merged.astype(BF16), w_o_ref[...])
    _ffn_rows(x_ref, gain_ref, wg_ref, wu_ref, wd_ref, o_ref, act_ref)
    y = o_ref[...]
    n, r = _prenorm(y, ple_gain_ref[...])
    gate = _sigmoid(_dot(n, w_gate_ref[...]) * r)
    y = y + gate * _dot(p_ref[...].astype(BF16), w_proj_ref[...])
    o_ref[...] = _rmsnorm(y, final_gain_ref[...]) if final else y


def _mix_ffn_ple(h, mc, ga, attn, p, gains, ple_gains, final_gain, w_attn_out, w_o, wg, wu, wd, w_gate, w_proj,
                 layer, final):
    t, d = h.shape
    d_ff = wg.shape[2]
    aw = attn.shape[1]
    pd = p.shape[2]
    tile = min(TOKEN_TILE, t)

    def rows(width):
        return pl.BlockSpec((tile, width), lambda i: (i, 0))

    hbm = pl.BlockSpec(memory_space=pl.ANY)
    return pl.pallas_call(
        functools.partial(_mix_ffn_ple_kernel, layer=layer, final=final),
        grid=(t // tile,),
        in_specs=[rows(d), rows(d), rows(d), rows(aw), pl.BlockSpec((None, tile, pd), lambda i: (layer, i, 0)),
                  _layer(gains.shape, layer), _layer(ple_gains.shape, layer), _resident(final_gain.shape)]
                 + [hbm] * 7,
        out_specs=rows(d),
        out_shape=jax.ShapeDtypeStruct((t, d), F32),
        scratch_shapes=[
            pltpu.VMEM((aw, d), BF16), pltpu.VMEM((d, d), BF16), pltpu.VMEM((d, d_ff), BF16),
            pltpu.VMEM((d, d_ff), BF16), pltpu.VMEM((d_ff, d), BF16), pltpu.VMEM((d, d), BF16),
            pltpu.VMEM((pd, d), BF16),
            pltpu.VMEM((STAGE_DEPTH, d, FF_CHUNK), F32), pltpu.VMEM((STAGE_DEPTH, STAGE_ROWS, d), F32),
            pltpu.SemaphoreType.DMA((STAGE_DEPTH,)), pltpu.SemaphoreType.DMA((STAGE_DEPTH,)),
            pltpu.VMEM((tile, d_ff), BF16), pltpu.VMEM((tile, d), F32),
        ],
        compiler_params=_params("arbitrary"),
        name="mix_ffn_ple",
    )(h, mc, ga, attn, p, gains, ple_gains, final_gain, w_attn_out, w_o, wg, wu, wd, w_gate, w_proj)


def _mix_in_kernel(h_ref, gain_ref, w_in_ref, conv_w_ref, w_conv_out_ref, cos_ref, sin_ref,
                   qt_ref, k_ref, vt_ref, mc_ref, ga_ref, u_ref, *, conv_width, attn_width):
    tile = h_ref.shape[1]
    cw, aw = conv_width, attn_width
    d = h_ref.shape[2]
    @pl.when(pl.program_id(1) == 0)
    def _():
        u_ref[0:SUBLANES, :] = jnp.zeros((SUBLANES, cw), F32)

    n, r = _prenorm(h_ref[0], gain_ref[...])

    def proj(col0, width):
        return _dot(n, _bf16(w_in_ref, cols=slice(col0, col0 + width))) * r

    b_gate, c_gate, xc = proj(0, cw), proj(cw, cw), proj(2 * cw, cw)
    u = c_gate * xc
    u_ref[SUBLANES:SUBLANES + tile, :] = u

    cos = cos_ref[0]
    sin = sin_ref[0]
    lane = lax.broadcasted_iota(jnp.int32, (1, LANES), 1)
    first_half = (lane % HEAD_DIM) < HALF_DIM
    scale = HEAD_DIM ** -0.5 * LOG2_E
    dim_row = lax.broadcasted_iota(jnp.int32, (LANES, 1), 0)
    blocks = range(tile // MOBA_BLOCK)

    def rotary(xg):
        partner = jnp.where(first_half, pltpu.roll(xg, LANES - HALF_DIM, 1), pltpu.roll(xg, HALF_DIM, 1))
        return xg * cos + partner * sin

    q_rows, k_rows, v_rows = proj(3 * cw, aw), proj(3 * cw + aw, aw), proj(3 * cw + 2 * aw, aw)
    for g in range(aw // LANES):
        q_t = (rotary(q_rows[:, g * LANES:(g + 1) * LANES]) * scale).T
        for hh in range(HEADS_PER_GROUP):
            q_th = jnp.where((dim_row // HEAD_DIM) == hh, q_t, 0.0).astype(qt_ref.dtype)
            for j in blocks:
                qt_ref[0, j, g * HEADS_PER_GROUP + hh] = q_th[:, j * MOBA_BLOCK:(j + 1) * MOBA_BLOCK]
    for g in range(aw // LANES):
        cols = slice(g * LANES, (g + 1) * LANES)
        k_ref[0, :, cols] = rotary(k_rows[:, cols]).astype(k_ref.dtype)
    ones =jnp.ones((V_ROWS - HEAD_DIM, MOBA_BLOCK), vt_ref.dtype)
    for g in range(aw // LANES):
        v_t = v_rows[:, g * LANES:(g + 1) * LANES].T.astype(vt_ref.dtype)
        for hh in range(HEADS_PER_GROUP):
            for j in blocks:
                head = g * HEADS_PER_GROUP + hh
                vt_ref[0, j, head, 0:HEAD_DIM, :] = v_t[hh * HEAD_DIM:(hh + 1) * HEAD_DIM,
                                                        j * MOBA_BLOCK:(j + 1) * MOBA_BLOCK]
                vt_ref[0, j, head, HEAD_DIM:V_ROWS, :] = ones
    base = 3 * cw + 3 * aw
    for c in range(d // PROJ_CHUNK):
        cols = slice(c * PROJ_CHUNK, (c + 1) * PROJ_CHUNK)
        ga_ref[0, :, cols] = _sigmoid(proj(base + d + c * PROJ_CHUNK, PROJ_CHUNK)).astype(ga_ref.dtype)
    conv_gate = [_sigmoid(proj(base + c * PROJ_CHUNK, PROJ_CHUNK)) for c in range(d // PROJ_CHUNK)]

    u_prev1 = u_ref[SUBLANES - 1:SUBLANES - 1 + tile, :]
    u_prev2 = u_ref[SUBLANES - 2:SUBLANES - 2 + tile, :]
    conv_w = conv_w_ref[...]
    conv = u_prev2 * conv_w[0:1, :] + u_prev1 * conv_w[1:2, :] + u * conv_w[2:3, :]
    u_ref[0:SUBLANES, :] = u[tile - SUBLANES:tile, :]
    gated = (b_gate * conv).astype(BF16)
    for c in range(d // PROJ_CHUNK):
        cols = slice(c * PROJ_CHUNK, (c + 1) * PROJ_CHUNK)
        y_conv = _dot(gated, _bf16(w_conv_out_ref, cols=cols))
        mc_ref[0, :, cols] = (conv_gate[c] * y_conv).astype(mc_ref.dtype)


def _mix_in(h, gains, w_in, conv_w, w_conv_out, cos, sin, layer):
    b, s, d = h.shape
    cw = conv_w.shape[2]
    aw = N_HEADS * HEAD_DIM
    tile = min(TOKEN_TILE, s)

    n_blocks = s // MOBA_BLOCK

    def rows(width):
        return pl.BlockSpec((1, tile, width), lambda i, j: (i, j, 0))

    def per_block(height):
        return pl.BlockSpec((1, tile // MOBA_BLOCK, N_HEADS, height, MOBA_BLOCK), lambda i, j: (i, j, 0, 0, 0))

    return pl.pallas_call(
        functools.partial(_mix_in_kernel, conv_width=cw, attn_width=aw),
        grid=(b, s // tile),
        in_specs=[rows(d), _layer(gains.shape, layer), _layer(w_in.shape, layer), _layer(conv_w.shape, layer),
                  _layer(w_conv_out.shape, layer), rows(LANES), rows(LANES)],
        out_specs=[per_block(LANES), rows(aw), per_block(V_ROWS), rows(d), rows(d)],
        out_shape=[jax.ShapeDtypeStruct((b, n_blocks, N_HEADS, LANES, MOBA_BLOCK), BF16),
                   jax.ShapeDtypeStruct((b, s, aw), BF16),
                   jax.ShapeDtypeStruct((b, n_blocks, N_HEADS, V_ROWS, MOBA_BLOCK), BF16),
                   jax.ShapeDtypeStruct((b, s, d), BF16), jax.ShapeDtypeStruct((b, s, d), BF16)],
        scratch_shapes=[pltpu.VMEM((tile + SUBLANES, cw), F32)],
        compiler_params=_params("parallel", "arbitrary"),
        name="mix_in",
    )(h, gains, w_in, conv_w, w_conv_out, cos, sin)


def _moba_kernel(qt_ref, k_ref, vt_ref, o_ref, kmean_ref, selb_ref, s0_ref, stat0_ref, s1_ref, stat1_ref, m_ref,
                 l_ref, acc_ref, *, n_blocks):
    group = pl.program_id(1)
    blk = MOBA_BLOCK
    groups = N_HEADS // HEADS_PER_GROUP

    @pl.when(group == 0)
    def _():
        for n in range(n_blocks):
            rows = slice(n * blk, (n + 1) * blk)
            kmean_ref[n:n + 1, :] = jnp.mean(k_ref[0, rows, :].astype(F32), axis=0, keepdims=True)

    blk_row = lax.broadcasted_iota(jnp.int32, (n_blocks, 1), 0)
    blk_row_f = blk_row.astype(F32)

    def select_blocks(tile, qi):
        is_past = blk_row < qi
        for g in range(groups):
            kmean = kmean_ref[:, g * LANES:(g + 1) * LANES]
            kmean_hi = kmean.astype(BF16)
            kmean_lo = (kmean - kmean_hi.astype(F32)).astype(BF16)
            for hh in range(HEADS_PER_GROUP):
                h = g * HEADS_PER_GROUP + hh
                q_th = qt_ref[0, tile, h]
                gate = _dot(kmean_hi, q_th) + _dot(kmean_lo, q_th)
                gate = jnp.where(is_past, gate, -jnp.inf)
                keep = jnp.full((n_blocks, blk), MASKED, F32)
                for _ in range(TOP_K):
                    best = jnp.max(gate, axis=0, keepdims=True)
                    first = jnp.min(jnp.where(gate == best, blk_row_f, float(n_blocks)), axis=0, keepdims=True)
                    pick = blk_row_f == first
                    keep = jnp.where(pick, 0.0, keep)
                    gate = jnp.where(pick, -jnp.inf, gate)
                selb_ref[tile, h] = jnp.where(is_past, keep, MASKED)

    slots = ((s0_ref, stat0_ref), (s1_ref, stat1_ref))

    def score_stage(tile, h, n, slot, diag):
        s_ref, stat_ref = slots[slot]
        g = h // HEADS_PER_GROUP
        start = pl.multiple_of(n * blk, blk)
        s_t = _dot(k_ref[0, pl.ds(start, blk), g * LANES:(g + 1) * LANES], qt_ref[0, tile, h])
        if diag:
            key = lax.broadcasted_iota(jnp.int32, (blk, blk), 0)
            qry = lax.broadcasted_iota(jnp.int32, (blk, blk), 1)
            s_t = jnp.where(key <= qry, s_t, MASKED)
            keep = jnp.zeros((1, blk), F32)
        else:
            keep = selb_ref[tile, h, pl.ds(n, 1), :]
        m_old = m_ref[tile, h]
        m_new = jnp.maximum(m_old, jnp.max(s_t, axis=0, keepdims=True) + keep)
        s_ref[h] = s_t
        m_ref[tile, h] = m_new
        stat_ref[h, 0:1, :] = m_new
        stat_ref[h, 1:2, :] = jnp.exp2(m_old - m_new)
        stat_ref[h, 2:3, :] = keep

    def value_stage(tile, h, n, slot):
        s_ref, stat_ref = slots[slot]
        shift = stat_ref[h, 0:1, :]
        alpha = stat_ref[h, 1:2, :]
        keep = stat_ref[h, 2:3, :]
        p_t = jnp.exp2((s_ref[h] - shift).astype(BF16))
        pv_t = _dot(vt_ref[0, n, h], p_t)
        pv_t = jnp.where(keep == 0.0, pv_t, 0.0)
        acc_ref[tile, h] = alpha * acc_ref[tile, h] + pv_t[:HEAD_DIM]
        l_ref[tile, h] = alpha * l_ref[tile, h] + pv_t[HEAD_DIM:HEAD_DIM + 1]

    def reset(tile):
        for h in range(N_HEADS):
            m_ref[tile, h] = jnp.full((1, blk), MASKED, F32)
            l_ref[tile, h] = jnp.zeros((1, blk), F32)
            acc_ref[tile, h] = jnp.zeros((HEAD_DIM, blk), F32)

    def step(tile, qi, n, waiting):
        prev = jnp.where(n == 0, qi, n - 1)
        for h in range(N_HEADS):
            score_stage(tile, h, n, 1 - waiting, False)
            value_stage(tile, h, prev, waiting)

    def write_out(tile):
        for g in range(groups):
            heads = [g * HEADS_PER_GROUP + hh for hh in range(HEADS_PER_GROUP)]
            o_t = jnp.concatenate([acc_ref[tile, h] / l_ref[tile, h] for h in heads], axis=0)
            o_ref[0, tile * blk:(tile + 1) * blk, g * LANES:(g + 1) * LANES] = o_t.T.astype(o_ref.dtype)

    base = MOBA_TILES * group
    pending = None
    first_slot = 0
    for tile in range(MOBA_TILES):
        qi = base + tile
        reset(tile)
        select_blocks(tile, qi)
        for h in range(N_HEADS):
            if pending is not None:
                value_stage(pending[0], h, pending[1], pending[2])
            score_stage(tile, h, qi, first_slot, True)
        if pending is not None:
            write_out(pending[0])

        def full_trip(t, carry, tile=tile, qi=qi, first_slot=first_slot):
            for i in range(MOBA_TILES):
                step(tile, qi, MOBA_TILES * t + i, (first_slot + i) % 2)
            return carry

        lax.fori_loop(0, group, full_trip, 0)
        for i in range(tile):
            step(tile, qi, base + i, (first_slot + i) % 2)
        last_slot = (first_slot + tile) % 2
        pending = (tile, jnp.where(qi == 0, qi, qi - 1), last_slot)
        first_slot = 1 - last_slot
    for h in range(N_HEADS):
        value_stage(pending[0], h, pending[1], pending[2])
    write_out(pending[0])


def _moba(qt, k, vt):
    b, s, aw = k.shape
    n_blocks = s // MOBA_BLOCK
    blk = MOBA_BLOCK
    tiles = MOBA_TILES
    assert n_blocks % tiles == 0 and tiles % 2 == 0
    return pl.pallas_call(
        functools.partial(_moba_kernel, n_blocks=n_blocks),
        grid=(b, n_blocks // tiles),
        in_specs=[pl.BlockSpec((1, tiles) + qt.shape[2:], lambda i, j: (i, j, 0, 0, 0)),
                  pl.BlockSpec((1, s, aw), lambda i, j: (i, 0, 0)),
                  pl.BlockSpec((1,) + vt.shape[1:], lambda i, j: (i, 0, 0, 0, 0))],
        out_specs=pl.BlockSpec((1, tiles * blk, aw), lambda i, j: (i, j, 0)),
        out_shape=jax.ShapeDtypeStruct((b, s, aw), BF16),
        scratch_shapes=[
            pltpu.VMEM((n_blocks, aw), F32),
            pltpu.VMEM((tiles, N_HEADS, n_blocks, blk), F32),
            pltpu.VMEM((N_HEADS, blk, blk), F32),
            pltpu.VMEM((N_HEADS, SUBLANES, blk), F32),
            pltpu.VMEM((N_HEADS, blk, blk), F32),
            pltpu.VMEM((N_HEADS, SUBLANES, blk), F32),
            pltpu.VMEM((tiles, N_HEADS, 1, blk), F32),
            pltpu.VMEM((tiles, N_HEADS, 1, blk), F32),
            pltpu.VMEM((tiles, N_HEADS, HEAD_DIM, blk), F32),
        ],
        compiler_params=_params("parallel", "arbitrary"),
        name="moba",
    )(qt, k, vt)


def kernel(x, p, positions, w_in, conv_w, w_conv_out, w_attn_out, w_o, ffn1_gate, ffn1_up, ffn1_down,
           ffn2_gate, ffn2_up, ffn2_down, norm_ffn1, norm_mix, norm_ffn2, norm_ple, w_ple_gate, w_ple_proj,
           norm_final):
    b, s, d = x.shape
    depth = w_in.shape[0]
    t = b * s
    assert s % MOBA_BLOCK == 0 and s % TOKEN_TILE == 0

    freq = jnp.arange(HALF_DIM, dtype=F32) / HALF_DIM
    inv_freq = jnp.tile(ROPE_THETA ** (-freq), LANES // HALF_DIM).reshape(1, LANES)

    def gains(g):
        return g.reshape(depth, 1, d)

    p_rows = p.reshape(depth, t, p.shape[-1])
    h = x.reshape(t, d)
    for i in range(depth):
        if i == 0:
            h, cos, sin = _ffn(h, gains(norm_ffn1), ffn1_gate, ffn1_up, ffn1_down, i,
                               rope=(positions.reshape(t, 1), inv_freq))
            cos, sin = cos.reshape(b, s, LANES), sin.reshape(b, s, LANES)
        else:
            h = _ffn(h, gains(norm_ffn1), ffn1_gate, ffn1_up, ffn1_down, i)
        qt, k, vt, mc, ga = _mix_in(h.reshape(b, s, d), gains(norm_mix), w_in, conv_w, w_conv_out, cos, sin, i)
        o = _moba(qt, k, vt)
        h = _mix_ffn_ple(h, mc.reshape(t, d), ga.reshape(t, d), o.reshape(t, -1), p_rows, gains(norm_ffn2),
                         gains(norm_ple), norm_final.reshape(1, d), w_attn_out, w_o, ffn2_gate, ffn2_up,
                         ffn2_down, w_ple_gate, w_ple_proj, i, i == depth - 1)
    return h.reshape(b, s, d)
```

```python
import functools

import jax
import jax.numpy as jnp
from jax import lax
from jax.experimental import pallas as pl
from jax.experimental.pallas import tpu as pltpu

N_HEADS = 8
HEAD_DIM = 64
HALF_DIM = HEAD_DIM // 2
MOBA_BLOCK = 256
TOP_K = 3
MOBA_TILES = 4
ROPE_THETA = 10000.0
EPS = 1e-6

LANES = 128
SUBLANES = 8
HEADS_PER_GROUP = LANES // HEAD_DIM
BF16_SUBLANES = 2 * SUBLANES
V_ROWS = HEAD_DIM + BF16_SUBLANES
VMEM_LIMIT_BYTES = 56 * 1024 * 1024
MASKED = -1e30
LOG2_E = 1.4426950408889634

TOKEN_TILE = 512
STAGE_ROWS = 256
STAGE_DEPTH = 5
FF_CHUNK = 256
OUT_CHUNK = 256
PROJ_CHUNK = 512

F32 = jnp.float32
BF16 = jnp.bfloat16


def _dot(a, b):
    return jnp.dot(a, b, preferred_element_type=F32)


def _rmsnorm(x, gain):
    ms = jnp.mean(x * x, axis=-1, keepdims=True)
    return x * lax.rsqrt(ms + EPS) * gain


def _prenorm(x, gain):
    r = lax.rsqrt(jnp.mean(x * x, axis=-1, keepdims=True) + EPS)
    return (x * gain).astype(BF16), r


def _sigmoid(x):
    return 1.0 / (1.0 + jnp.exp(-x))


def _resident(shape):
    nd = len(shape)
    return pl.BlockSpec(shape, lambda *_: (0,) * nd, pipeline_mode=pl.Buffered(1))


def _layer(stacked_shape, layer):
    nd = len(stacked_shape) - 1
    return pl.BlockSpec((None,) + tuple(stacked_shape[1:]), lambda *_: (layer,) + (0,) * nd,
                        pipeline_mode=pl.Buffered(1))


def _bf16(ref, rows=slice(None), cols=slice(None)):
    return ref[rows, cols].astype(BF16)


def _params(*semantics):
    return pltpu.CompilerParams(dimension_semantics=semantics, vmem_limit_bytes=VMEM_LIMIT_BYTES)


def _ffn_rows(h_ref, gain_ref, wg_ref, wu_ref, wd_ref, o_ref, act_ref):
    x = h_ref[...]
    n, r = _prenorm(x, gain_ref[...])
    d_ff = wg_ref.shape[1]
    for c in range(d_ff // FF_CHUNK):
        cols = slice(c * FF_CHUNK, (c + 1) * FF_CHUNK)
        g = _dot(n, _bf16(wg_ref, cols=cols)) * r
        u = _dot(n, _bf16(wu_ref, cols=cols)) * r
        act_ref[:, cols] = (g * _sigmoid(g) * u).astype(BF16)
    for c in range(x.shape[1] // OUT_CHUNK):
        cols = slice(c * OUT_CHUNK, (c + 1) * OUT_CHUNK)
        o_ref[:, cols] = x[:, cols] + 0.5 * _dot(act_ref[...], _bf16(wd_ref, cols=cols))


def _ffn_rope_kernel(h_ref, gain_ref, wg_ref, wu_ref, wd_ref, pos_ref, inv_freq_ref, o_ref, cos_ref, sin_ref,
                     act_ref):
    ang = pos_ref[...].astype(F32) * inv_freq_ref[...]
    lane = lax.broadcasted_iota(jnp.int32, (1, LANES), 1)
    first_half = (lane % HEAD_DIM) < HALF_DIM
    cos_ref[...] = jnp.cos(ang)
    sin = jnp.sin(ang)
    sin_ref[...] = jnp.where(first_half, -sin, sin)
    _ffn_rows(h_ref, gain_ref, wg_ref, wu_ref, wd_ref, o_ref, act_ref)


def _ffn(h, gains, wg, wu, wd, layer, rope=None):
    t, d = h.shape
    d_ff = wg.shape[2]
    tile = min(TOKEN_TILE, t)

    def rows(width):
        return pl.BlockSpec((tile, width), lambda i: (i, 0))

    in_specs = [rows(d), _layer(gains.shape, layer), _layer(wg.shape, layer), _layer(wu.shape, layer),
                _layer(wd.shape, layer)]
    args = [h, gains, wg, wu, wd]
    out_specs, out_shape, body = rows(d), jax.ShapeDtypeStruct((t, d), F32), _ffn_rows
    if rope is not None:
        positions, inv_freq = rope
        in_specs += [rows(1), _resident(inv_freq.shape)]
        args += [positions, inv_freq]
        out_specs = [rows(d), rows(LANES), rows(LANES)]
        out_shape = [out_shape] + [jax.ShapeDtypeStruct((t, LANES), F32)] * 2
        body = _ffn_rope_kernel
    return pl.pallas_call(
        body,
        grid=(t // tile,),
        in_specs=in_specs,
        out_specs=out_specs,
        out_shape=out_shape,
        scratch_shapes=[pltpu.VMEM((tile, d_ff), BF16)],
        compiler_params=_params("parallel"),
        name="ffn" if rope is None else "ffn_rope",
    )(*args)


def _col_windows(src_hbm, layer, dst_ref, width):
    return [(src_hbm.at[layer, :, c:c + width], dst_ref.at[:, c:c + width])
            for c in range(0, dst_ref.shape[1], width)]


def _row_windows(src_hbm, layer, dst_ref, height):
    return [(src_hbm.at[layer, r:r + height, :], dst_ref.at[r:r + height, :])
            for r in range(0, dst_ref.shape[0], height)]


def _stage_params(jobs, ring_ref, sem_ref):
    depth = ring_ref.shape[0]

    def copy(i):
        return pltpu.make_async_copy(jobs[i][0], ring_ref.at[i % depth], sem_ref.at[i % depth])

    for i in range(min(depth - 1, len(jobs))):
        copy(i).start(priority=i % 2)
    for i, (_, dst) in enumerate(jobs):
        if i + depth - 1 < len(jobs):
            nxt = i + depth - 1
            copy(nxt).start(priority=nxt % 2)
        copy(i).wait()
        dst[...] = ring_ref[i % depth].astype(BF16)


def _mix_ffn_ple_kernel(h_ref, mc_ref, ga_ref, attn_ref, p_ref, gain_ref, ple_gain_ref, final_gain_ref,
                        w_attn_out_hbm, w_o_hbm, wg_hbm, wu_hbm, wd_hbm, w_gate_hbm, w_proj_hbm, o_ref,
                        w_attn_out_ref, w_o_ref, wg_ref, wu_ref, wd_ref, w_gate_ref, w_proj_ref,
                        col_ring, row_ring, col_sem, row_sem, act_ref, x_ref, *, layer, final):
    @pl.when(pl.program_id(0) == 0)
    def _():
        row_jobs = (_row_windows(w_attn_out_hbm, layer, w_attn_out_ref, STAGE_ROWS)
                    + _row_windows(wd_hbm, layer, wd_ref, STAGE_ROWS)
                    + _row_windows(w_proj_hbm, layer, w_proj_ref, STAGE_ROWS))
        col_jobs = _col_windows(w_o_hbm, layer, w_o_ref, FF_CHUNK)
        for gate_job, up_job in zip(_col_windows(wg_hbm, layer, wg_ref, FF_CHUNK),
                                    _col_windows(wu_hbm, layer, wu_ref, FF_CHUNK)):
            col_jobs += [gate_job, up_job]
        col_jobs += _col_windows(w_gate_hbm, layer, w_gate_ref, FF_CHUNK)
        _stage_params(row_jobs, row_ring, row_sem)
        _stage_params(col_jobs, col_ring, col_sem)

    y_attn = _dot(attn_ref[...], w_attn_out_ref[...])
    merged = mc_ref[...].astype(F32) + ga_ref[...].astype(F32) * y_attn
    x_ref[...] = h_ref[...] + _dot(merged.astype(BF16), w_o_ref[...])
    _ffn_rows(x_ref, gain_ref, wg_ref, wu_ref, wd_ref, o_ref, act_ref)
    y = o_ref[...]
    n, r = _prenorm(y, ple_gain_ref[...])
    gate = _sigmoid(_dot(n, w_gate_ref[...]) * r)
    y = y + gate * _dot(p_ref[...].astype(BF16), w_proj_ref[...])
    o_ref[...] = _rmsnorm(y, final_gain_ref[...]) if final else y


def _mix_ffn_ple(h, mc, ga, attn, p, gains, ple_gains, final_gain, w_attn_out, w_o, wg, wu, wd, w_gate, w_proj,
                 layer, final):
    t, d = h.shape
    d_ff = wg.shape[2]
    aw = attn.shape[1]
    pd = p.shape[2]
    tile = min(TOKEN_TILE, t)

    def rows(width):
        return pl.BlockSpec((tile, width), lambda i: (i, 0))

    hbm = pl.BlockSpec(memory_space=pl.ANY)
    return pl.pallas_call(
        functools.partial(_mix_ffn_ple_kernel, layer=layer, final=final),
        grid=(t // tile,),
        in_specs=[rows(d), rows(d), rows(d), rows(aw), pl.BlockSpec((None, tile, pd), lambda i: (layer, i, 0)),
                  _layer(gains.shape, layer), _layer(ple_gains.shape, layer), _resident(final_gain.shape)]
                 + [hbm] * 7,
        out_specs=rows(d),
        out_shape=jax.ShapeDtypeStruct((t, d), F32),
        scratch_shapes=[
            pltpu.VMEM((aw, d), BF16), pltpu.VMEM((d, d), BF16), pltpu.VMEM((d, d_ff), BF16),
            pltpu.VMEM((d, d_ff), BF16), pltpu.VMEM((d_ff, d), BF16), pltpu.VMEM((d, d), BF16),
            pltpu.VMEM((pd, d), BF16),
            pltpu.VMEM((STAGE_DEPTH, d, FF_CHUNK), F32), pltpu.VMEM((STAGE_DEPTH, STAGE_ROWS, d), F32),
            pltpu.SemaphoreType.DMA((STAGE_DEPTH,)), pltpu.SemaphoreType.DMA((STAGE_DEPTH,)),
            pltpu.VMEM((tile, d_ff), BF16), pltpu.VMEM((tile, d), F32),
        ],
        compiler_params=_params("arbitrary"),
        name="mix_ffn_ple",
    )(h, mc, ga, attn, p, gains, ple_gains, final_gain, w_attn_out, w_o, wg, wu, wd, w_gate, w_proj)


def _mix_in_kernel(h_ref, gain_ref, w_in_ref, conv_w_ref, w_conv_out_ref, cos_ref, sin_ref,
                   qt_ref, k_ref, vt_ref, mc_ref, ga_ref, u_ref, *, conv_width, attn_width):
    tile = h_ref.shape[1]
    cw, aw = conv_width, attn_width
    d = h_ref.shape[2]
    @pl.when(pl.program_id(1) == 0)
    def _():
        u_ref[0:SUBLANES, :] = jnp.zeros((SUBLANES, cw), F32)

    n, r = _prenorm(h_ref[0], gain_ref[...])

    def proj(col0, width):
        return _dot(n, _bf16(w_in_ref, cols=slice(col0, col0 + width))) * r

    b_gate, c_gate, xc = proj(0, cw), proj(cw, cw), proj(2 * cw, cw)
    u = c_gate * xc
    u_ref[SUBLANES:SUBLANES + tile, :] = u

    cos = cos_ref[0]
    sin = sin_ref[0]
    lane = lax.broadcasted_iota(jnp.int32, (1, LANES), 1)
    first_half = (lane % HEAD_DIM) < HALF_DIM
    scale = HEAD_DIM ** -0.5 * LOG2_E
    dim_row = lax.broadcasted_iota(jnp.int32, (LANES, 1), 0)
    blocks = range(tile // MOBA_BLOCK)

    def rotary(xg):
        partner = jnp.where(first_half, pltpu.roll(xg, LANES - HALF_DIM, 1), pltpu.roll(xg, HALF_DIM, 1))
        return xg * cos + partner * sin

    q_rows, k_rows, v_rows = proj(3 * cw, aw), proj(3 * cw + aw, aw), proj(3 * cw + 2 * aw, aw)
    for g in range(aw // LANES):
        q_t = (rotary(q_rows[:, g * LANES:(g + 1) * LANES]) * scale).T
        for hh in range(HEADS_PER_GROUP):
            q_th = jnp.where((dim_row // HEAD_DIM) == hh, q_t, 0.0).astype(qt_ref.dtype)
            for j in blocks:
                qt_ref[0, j, g * HEADS_PER_GROUP + hh] = q_th[:, j * MOBA_BLOCK:(j + 1) * MOBA_BLOCK]
    for g in range(aw // LANES):
        cols = slice(g * LANES, (g + 1) * LANES)
        k_ref[0, :, cols] = rotary(k_rows[:, cols]).astype(k_ref.dtype)
    ones =jnp.ones((V_ROWS - HEAD_DIM, MOBA_BLOCK), vt_ref.dtype)
    for g in range(aw // LANES):
        v_t = v_rows[:, g * LANES:(g + 1) * LANES].T.astype(vt_ref.dtype)
        for hh in range(HEADS_PER_GROUP):
            for j in blocks:
                head = g * HEADS_PER_GROUP + hh
                vt_ref[0, j, head, 0:HEAD_DIM, :] = v_t[hh * HEAD_DIM:(hh + 1) * HEAD_DIM,
                                                        j * MOBA_BLOCK:(j + 1) * MOBA_BLOCK]
                vt_ref[0, j, head, HEAD_DIM:V_ROWS, :] = ones
    base = 3 * cw + 3 * aw
    for c in range(d // PROJ_CHUNK):
        cols = slice(c * PROJ_CHUNK, (c + 1) * PROJ_CHUNK)
        ga_ref[0, :, cols] = _sigmoid(proj(base + d + c * PROJ_CHUNK, PROJ_CHUNK)).astype(ga_ref.dtype)
    conv_gate = [_sigmoid(proj(base + c * PROJ_CHUNK, PROJ_CHUNK)) for c in range(d // PROJ_CHUNK)]

    u_prev1 = u_ref[SUBLANES - 1:SUBLANES - 1 + tile, :]
    u_prev2 = u_ref[SUBLANES - 2:SUBLANES - 2 + tile, :]
    conv_w = conv_w_ref[...]
    conv = u_prev2 * conv_w[0:1, :] + u_prev1 * conv_w[1:2, :] + u * conv_w[2:3, :]
    u_ref[0:SUBLANES, :] = u[tile - SUBLANES:tile, :]
    gated = (b_gate * conv).astype(BF16)
    for c in range(d // PROJ_CHUNK):
        cols = slice(c * PROJ_CHUNK, (c + 1) * PROJ_CHUNK)
        y_conv = _dot(gated, _bf16(w_conv_out_ref, cols=cols))
        mc_ref[0, :, cols] = (conv_gate[c] * y_conv).astype(mc_ref.dtype)


def _mix_in(h, gains, w_in, conv_w, w_conv_out, cos, sin, layer):
    b, s, d = h.shape
    cw = conv_w.shape[2]
    aw = N_HEADS * HEAD_DIM
    tile = min(TOKEN_TILE, s)

    n_blocks = s // MOBA_BLOCK

    def rows(width):
        return pl.BlockSpec((1, tile, width), lambda i, j: (i, j, 0))

    def per_block(height):
        return pl.BlockSpec((1, tile // MOBA_BLOCK, N_HEADS, height, MOBA_BLOCK), lambda i, j: (i, j, 0, 0, 0))

    return pl.pallas_call(
        functools.partial(_mix_in_kernel, conv_width=cw, attn_width=aw),
        grid=(b, s // tile),
        in_specs=[rows(d), _layer(gains.shape, layer), _layer(w_in.shape, layer), _layer(conv_w.shape, layer),
                  _layer(w_conv_out.shape, layer), rows(LANES), rows(LANES)],
        out_specs=[per_block(LANES), rows(aw), per_block(V_ROWS), rows(d), rows(d)],
        out_shape=[jax.ShapeDtypeStruct((b, n_blocks, N_HEADS, LANES, MOBA_BLOCK), BF16),
                   jax.ShapeDtypeStruct((b, s, aw), BF16),
                   jax.ShapeDtypeStruct((b, n_blocks, N_HEADS, V_ROWS, MOBA_BLOCK), BF16),
                   jax.ShapeDtypeStruct((b, s, d), BF16), jax.ShapeDtypeStruct((b, s, d), BF16)],
        scratch_shapes=[pltpu.VMEM((tile + SUBLANES, cw), F32)],
        compiler_params=_params("parallel", "arbitrary"),
        name="mix_in",
    )(h, gains, w_in, conv_w, w_conv_out, cos, sin)


def _moba_kernel(qt_ref, k_ref, vt_ref, o_ref, kmean_ref, selb_ref, s0_ref, stat0_ref, s1_ref, stat1_ref, m_ref,
                 l_ref, acc_ref, *, n_blocks):
    group = pl.program_id(1)
    blk = MOBA_BLOCK
    groups = N_HEADS // HEADS_PER_GROUP

    @pl.when(group == 0)
    def _():
        for n in range(n_blocks):
            rows = slice(n * blk, (n + 1) * blk)
            kmean_ref[n:n + 1, :] = jnp.mean(k_ref[0, rows, :].astype(F32), axis=0, keepdims=True)

    blk_row = lax.broadcasted_iota(jnp.int32, (n_blocks, 1), 0)
    blk_row_f = blk_row.astype(F32)

    def select_blocks(tile, qi):
        is_past = blk_row < qi
        for g in range(groups):
            kmean = kmean_ref[:, g * LANES:(g + 1) * LANES]
            kmean_hi = kmean.astype(BF16)
            kmean_lo = (kmean - kmean_hi.astype(F32)).astype(BF16)
            kmean_parts = jnp.concatenate([kmean_hi, kmean_lo], axis=0)
            for hh in range(HEADS_PER_GROUP):
                h = g * HEADS_PER_GROUP + hh
                q_th = qt_ref[0, tile, h]
                parts = _dot(kmean_parts, q_th)
                gate = parts[:n_blocks] + parts[n_blocks:]
                gate = jnp.where(is_past, gate, -jnp.inf)
                keep = jnp.full((n_blocks, blk), MASKED, F32)
                for _ in range(TOP_K):
                    best = jnp.max(gate, axis=0, keepdims=True)
                    first = jnp.min(jnp.where(gate == best, blk_row_f, float(n_blocks)), axis=0, keepdims=True)
                    pick = blk_row_f == first
                    keep = jnp.where(pick, 0.0, keep)
                    gate = jnp.where(pick, -jnp.inf, gate)
                selb_ref[tile, h] = jnp.where(is_past, keep, MASKED)

    slots = ((s0_ref, stat0_ref), (s1_ref, stat1_ref))

    def score_stage(tile, h, n, slot, diag):
        s_ref, stat_ref = slots[slot]
        g = h // HEADS_PER_GROUP
        start = pl.multiple_of(n * blk, blk)
        s_t = _dot(k_ref[0, pl.ds(start, blk), g * LANES:(g + 1) * LANES], qt_ref[0, tile, h])
        if diag:
            key = lax.broadcasted_iota(jnp.int32, (blk, blk), 0)
            qry = lax.broadcasted_iota(jnp.int32, (blk, blk), 1)
            s_t = jnp.where(key <= qry, s_t, MASKED)
            keep = jnp.zeros((1, blk), F32)
        else:
            keep = selb_ref[tile, h, pl.ds(n, 1), :]
        m_old = m_ref[tile, h]
        m_new = jnp.maximum(m_old, jnp.max(s_t, axis=0, keepdims=True) + keep)
        s_ref[h] = s_t
        m_ref[tile, h] = m_new
        stat_ref[h, 0:1, :] = m_new
        stat_ref[h, 1:2, :] = jnp.exp2(m_old - m_new)
        stat_ref[h, 2:3, :] = keep

    def value_stage(tile, h, n, slot):
        s_ref, stat_ref = slots[slot]
        shift = stat_ref[h, 0:1, :]
        alpha = stat_ref[h, 1:2, :]
        keep = stat_ref[h, 2:3, :]
        p_t = jnp.exp2(s_ref[h] - shift).astype(BF16)
        pv_t = _dot(vt_ref[0, n, h], p_t)
        pv_t = jnp.where(keep == 0.0, pv_t, 0.0)
        acc_ref[tile, h] = alpha * acc_ref[tile, h] + pv_t[:HEAD_DIM]
        l_ref[tile, h] = alpha * l_ref[tile, h] + pv_t[HEAD_DIM:HEAD_DIM + 1]

    def reset(tile):
        for h in range(N_HEADS):
            m_ref[tile, h] = jnp.full((1, blk), MASKED, F32)
            l_ref[tile, h] = jnp.zeros((1, blk), F32)
            acc_ref[tile, h] = jnp.zeros((HEAD_DIM, blk), F32)

    def step(tile, qi, n, waiting):
        prev = jnp.where(n == 0, qi, n - 1)
        for h in range(N_HEADS):
            score_stage(tile, h, n, 1 - waiting, False)
            value_stage(tile, h, prev, waiting)

    def write_out(tile):
        for g in range(groups):
            heads = [g * HEADS_PER_GROUP + hh for hh in range(HEADS_PER_GROUP)]
            o_t = jnp.concatenate([acc_ref[tile, h] / l_ref[tile, h] for h in heads], axis=0)
            o_ref[0, tile * blk:(tile + 1) * blk, g * LANES:(g + 1) * LANES] = o_t.T.astype(o_ref.dtype)

    base = MOBA_TILES * group
    pending = None
    first_slot = 0
    for tile in range(MOBA_TILES):
        qi = base + tile
        reset(tile)
        select_blocks(tile, qi)
        for h in range(N_HEADS):
            if pending is not None:
                value_stage(pending[0], h, pending[1], pending[2])
            score_stage(tile, h, qi, first_slot, True)
        if pending is not None:
            write_out(pending[0])

        def full_trip(t, carry, tile=tile, qi=qi, first_slot=first_slot):
            for i in range(MOBA_TILES):
                step(tile, qi, MOBA_TILES * t + i, (first_slot + i) % 2)
            return carry

        lax.fori_loop(0, group, full_trip, 0)
        for i in range(tile):
            step(tile, qi, base + i, (first_slot + i) % 2)
        last_slot = (first_slot + tile) % 2
        pending = (tile, jnp.where(qi == 0, qi, qi - 1), last_slot)
        first_slot = 1 - last_slot
    for h in range(N_HEADS):
        value_stage(pending[0], h, pending[1], pending[2])
    write_out(pending[0])


def _moba(qt, k, vt):
    b, s, aw = k.shape
    n_blocks = s // MOBA_BLOCK
    blk = MOBA_BLOCK
    tiles = MOBA_TILES
    assert n_blocks % tiles == 0 and tiles % 2 == 0
    return pl.pallas_call(
        functools.partial(_moba_kernel, n_blocks=n_blocks),
        grid=(b, n_blocks // tiles),
        in_specs=[pl.BlockSpec((1, tiles) + qt.shape[2:], lambda i, j: (i, j, 0, 0, 0)),
                  pl.BlockSpec((1, s, aw), lambda i, j: (i, 0, 0)),
                  pl.BlockSpec((1,) + vt.shape[1:], lambda i, j: (i, 0, 0, 0, 0))],
        out_specs=pl.BlockSpec((1, tiles * blk, aw), lambda i, j: (i, j, 0)),
        out_shape=jax.ShapeDtypeStruct((b, s, aw), BF16),
        scratch_shapes=[
            pltpu.VMEM((n_blocks, aw), F32),
            pltpu.VMEM((tiles, N_HEADS, n_blocks, blk), F32),
            pltpu.VMEM((N_HEADS, blk, blk), F32),
            pltpu.VMEM((N_HEADS, SUBLANES, blk), F32),
            pltpu.VMEM((N_HEADS, blk, blk), F32),
            pltpu.VMEM((N_HEADS, SUBLANES, blk), F32),
            pltpu.VMEM((tiles, N_HEADS, 1, blk), F32),
            pltpu.VMEM((tiles, N_HEADS, 1, blk), F32),
            pltpu.VMEM((tiles, N_HEADS, HEAD_DIM, blk), F32),
        ],
        compiler_params=_params("parallel", "arbitrary"),
        name="moba",
    )(qt, k, vt)


def kernel(x, p, positions, w_in, conv_w, w_conv_out, w_attn_out, w_o, ffn1_gate, ffn1_up, ffn1_down,
           ffn2_gate, ffn2_up, ffn2_down, norm_ffn1, norm_mix, norm_ffn2, norm_ple, w_ple_gate, w_ple_proj,
           norm_final):
    b, s, d = x.shape
    depth = w_in.shape[0]
    t = b * s
    assert s % MOBA_BLOCK == 0 and s % TOKEN_TILE == 0

    freq = jnp.arange(HALF_DIM, dtype=F32) / HALF_DIM
    inv_freq = jnp.tile(ROPE_THETA ** (-freq), LANES // HALF_DIM).reshape(1, LANES)

    def gains(g):
        return g.reshape(depth, 1, d)

    p_rows = p.reshape(depth, t, p.shape[-1])
    h = x.reshape(t, d)
    for i in range(depth):
        if i == 0:
            h, cos, sin = _ffn(h, gains(norm_ffn1), ffn1_gate, ffn1_up, ffn1_down, i,
                               rope=(positions.reshape(t, 1), inv_freq))
            cos, sin = cos.reshape(b, s, LANES), sin.reshape(b, s, LANES)
        else:
            h = _ffn(h, gains(norm_ffn1), ffn1_gate, ffn1_up, ffn1_down, i)
        qt, k, vt, mc, ga = _mix_in(h.reshape(b, s, d), gains(norm_mix), w_in, conv_w, w_conv_out, cos, sin, i)
        o = _moba(qt, k, vt)
        h = _mix_ffn_ple(h, mc.reshape(t, d), ga.reshape(t, d), o.reshape(t, -1), p_rows, gains(norm_ffn2),
                         gains(norm_ple), norm_final.reshape(1, d), w_attn_out, w_o, ffn2_gate, ffn2_up,
                         ffn2_down, w_ple_gate, w_ple_proj, i, i == depth - 1)
    return h.reshape(b, s, d)
```

```python
import functools

import jax
import jax.numpy as jnp
from jax import lax
from jax.experimental import pallas as pl
from jax.experimental.pallas import tpu as pltpu

N_HEADS = 8
HEAD_DIM = 64
HALF_DIM = HEAD_DIM // 2
MOBA_BLOCK = 256
TOP_K = 3
MOBA_TILES = 4
ROPE_THETA = 10000.0
EPS = 1e-6

LANES = 128
SUBLANES = 8
HEADS_PER_GROUP = LANES // HEAD_DIM
BF16_SUBLANES = 2 * SUBLANES
V_ROWS = HEAD_DIM + BF16_SUBLANES
VMEM_LIMIT_BYTES = 56 * 1024 * 1024
MASKED = -1e30
LOG2_E = 1.4426950408889634

TOKEN_TILE = 512
STAGE_ROWS = 256
STAGE_DEPTH = 5
FF_CHUNK = 256
OUT_CHUNK = 256
PROJ_CHUNK = 512

F32 = jnp.float32
BF16 = jnp.bfloat16


def _dot(a, b):
    return jnp.dot(a, b, preferred_element_type=F32)


def _rmsnorm(x, gain):
    ms = jnp.mean(x * x, axis=-1, keepdims=True)
    return x * lax.rsqrt(ms + EPS) * gain


def _prenorm(x, gain):
    r = lax.rsqrt(jnp.mean(x * x, axis=-1, keepdims=True) + EPS)
    return (x * gain).astype(BF16), r


def _sigmoid(x):
    return 1.0 / (1.0 + jnp.exp(-x))


def _resident(shape):
    nd = len(shape)
    return pl.BlockSpec(shape, lambda *_: (0,) * nd, pipeline_mode=pl.Buffered(1))


def _layer(stacked_shape, layer):
    nd = len(stacked_shape) - 1
    return pl.BlockSpec((None,) + tuple(stacked_shape[1:]), lambda *_: (layer,) + (0,) * nd,
                        pipeline_mode=pl.Buffered(1))


def _bf16(ref, rows=slice(None), cols=slice(None)):
    return ref[rows, cols].astype(BF16)


def _params(*semantics):
    return pltpu.CompilerParams(dimension_semantics=semantics, vmem_limit_bytes=VMEM_LIMIT_BYTES)


def _ffn_rows(h_ref, gain_ref, wg_ref, wu_ref, wd_ref, o_ref, act_ref):
    x = h_ref[...]
    n, r = _prenorm(x, gain_ref[...])
    d_ff = wg_ref.shape[1]
    for c in range(d_ff // FF_CHUNK):
        cols = slice(c * FF_CHUNK, (c + 1) * FF_CHUNK)
        g = _dot(n, _bf16(wg_ref, cols=cols)) * r
        u = _dot(n, _bf16(wu_ref, cols=cols)) * r
        act_ref[:, cols] = (g * _sigmoid(g) * u).astype(BF16)
    for c in range(x.shape[1] // OUT_CHUNK):
        cols = slice(c * OUT_CHUNK, (c + 1) * OUT_CHUNK)
        o_ref[:, cols] = x[:, cols] + 0.5 * _dot(act_ref[...], _bf16(wd_ref, cols=cols))


def _ffn_rope_kernel(h_ref, gain_ref, wg_ref, wu_ref, wd_ref, pos_ref, inv_freq_ref, o_ref, cos_ref, sin_ref,
                     act_ref):
    ang = pos_ref[...].astype(F32) * inv_freq_ref[...]
    lane = lax.broadcasted_iota(jnp.int32, (1, LANES), 1)
    first_half = (lane % HEAD_DIM) < HALF_DIM
    cos_ref[...] = jnp.cos(ang)
    sin = jnp.sin(ang)
    sin_ref[...] = jnp.where(first_half, -sin, sin)
    _ffn_rows(h_ref, gain_ref, wg_ref, wu_ref, wd_ref, o_ref, act_ref)


def _ffn(h, gains, wg, wu, wd, layer, rope=None):
    t, d = h.shape
    d_ff = wg.shape[2]
    tile = min(TOKEN_TILE, t)

    def rows(width):
        return pl.BlockSpec((tile, width), lambda i: (i, 0))

    in_specs = [rows(d), _layer(gains.shape, layer), _layer(wg.shape, layer), _layer(wu.shape, layer),
                _layer(wd.shape, layer)]
    args = [h, gains, wg, wu, wd]
    out_specs, out_shape, body = rows(d), jax.ShapeDtypeStruct((t, d), F32), _ffn_rows
    if rope is not None:
        positions, inv_freq = rope
        in_specs += [rows(1), _resident(inv_freq.shape)]
        args += [positions, inv_freq]
        out_specs = [rows(d), rows(LANES), rows(LANES)]
        out_shape = [out_shape] + [jax.ShapeDtypeStruct((t, LANES), F32)] * 2
        body = _ffn_rope_kernel
    return pl.pallas_call(
        body,
        grid=(t // tile,),
        in_specs=in_specs,
        out_specs=out_specs,
        out_shape=out_shape,
        scratch_shapes=[pltpu.VMEM((tile, d_ff), BF16)],
        compiler_params=_params("parallel"),
        name="ffn" if rope is None else "ffn_rope",
    )(*args)


def _col_windows(src_hbm, layer, dst_ref, width):
    return [(src_hbm.at[layer, :, c:c + width], dst_ref.at[:, c:c + width])
            for c in range(0, dst_ref.shape[1], width)]


def _row_windows(src_hbm, layer, dst_ref, height):
    return [(src_hbm.at[layer, r:r + height, :], dst_ref.at[r:r + height, :])
            for r in range(0, dst_ref.shape[0], height)]


def _prime_ring(jobs, ring_ref, sem_ref):
    depth = ring_ref.shape[0]

    def copy(i):
        return pltpu.make_async_copy(jobs[i][0], ring_ref.at[i % depth], sem_ref.at[i % depth])

    for i in range(min(depth - 1, len(jobs))):
        copy(i).start()
    return copy


def _drain_ring(jobs, ring_ref, copy):
    depth = ring_ref.shape[0]
    for i, (_, dst) in enumerate(jobs):
        if i + depth - 1 < len(jobs):
            copy(i + depth - 1).start()
        copy(i).wait()
        dst[...] = ring_ref[i % depth].astype(BF16)


def _mix_ffn_ple_kernel(h_ref, mc_ref, ga_ref, attn_ref, p_ref, gain_ref, ple_gain_ref, final_gain_ref,
                        w_attn_out_hbm, w_o_hbm, wg_hbm, wu_hbm, wd_hbm, w_gate_hbm, w_proj_hbm, o_ref,
                        w_attn_out_ref, w_o_ref, wg_ref, wu_ref, wd_ref, w_gate_ref, w_proj_ref,
                        col_ring, row_ring, col_sem, row_sem, act_ref, x_ref, *, layer, final):
    @pl.when(pl.program_id(0) == 0)
    def _():
        row_jobs = (_row_windows(w_attn_out_hbm, layer, w_attn_out_ref, STAGE_ROWS)
                    + _row_windows(wd_hbm, layer, wd_ref, STAGE_ROWS)
                    + _row_windows(w_proj_hbm, layer, w_proj_ref, STAGE_ROWS))
        col_jobs = _col_windows(w_o_hbm, layer, w_o_ref, FF_CHUNK)
        for gate_job, up_job in zip(_col_windows(wg_hbm, layer, wg_ref, FF_CHUNK),
                                    _col_windows(wu_hbm, layer, wu_ref, FF_CHUNK)):
            col_jobs += [gate_job, up_job]
        col_jobs += _col_windows(w_gate_hbm, layer, w_gate_ref, FF_CHUNK)
        row_copy, col_copy = _prime_ring(row_jobs, row_ring, row_sem), _prime_ring(col_jobs, col_ring, col_sem)
        _drain_ring(row_jobs, row_ring, row_copy)
        _drain_ring(col_jobs, col_ring, col_copy)

    y_attn = _dot(attn_ref[...], w_attn_out_ref[...])
    merged = mc_ref[...].astype(F32) + ga_ref[...].astype(F32) * y_attn
    x_ref[...] = h_ref[...] + _dot(merged.astype(BF16), w_o_ref[...])
    _ffn_rows(x_ref, gain_ref, wg_ref, wu_ref, wd_ref, o_ref, act_ref)
    y = o_ref[...]
    n, r = _prenorm(y, ple_gain_ref[...])
    gate = _sigmoid(_dot(n, w_gate_ref[...]) * r)
    y = y + gate * _dot(p_ref[...].astype(BF16), w_proj_ref[...])
    o_ref[...] = _rmsnorm(y, final_gain_ref[...]) if final else y


def _mix_ffn_ple(h, mc, ga, attn, p, gains, ple_gains, final_gain, w_attn_out, w_o, wg, wu, wd, w_gate, w_proj,
                 layer, final):
    t, d = h.shape
    d_ff = wg.shape[2]
    aw = attn.shape[1]
    pd = p.shape[2]
    tile = min(TOKEN_TILE, t)

    def rows(width):
        return pl.BlockSpec((tile, width), lambda i: (i, 0))

    hbm = pl.BlockSpec(memory_space=pl.ANY)
    return pl.pallas_call(
        functools.partial(_mix_ffn_ple_kernel, layer=layer, final=final),
        grid=(t // tile,),
        in_specs=[rows(d), rows(d), rows(d), rows(aw), pl.BlockSpec((None, tile, pd), lambda i: (layer, i, 0)),
                  _layer(gains.shape, layer), _layer(ple_gains.shape, layer), _resident(final_gain.shape)]
                 + [hbm] * 7,
        out_specs=rows(d),
        out_shape=jax.ShapeDtypeStruct((t, d), F32),
        scratch_shapes=[
            pltpu.VMEM((aw, d), BF16), pltpu.VMEM((d, d), BF16), pltpu.VMEM((d, d_ff), BF16),
            pltpu.VMEM((d, d_ff), BF16), pltpu.VMEM((d_ff, d), BF16), pltpu.VMEM((d, d), BF16),
            pltpu.VMEM((pd, d), BF16),
            pltpu.VMEM((STAGE_DEPTH, d, FF_CHUNK), F32), pltpu.VMEM((STAGE_DEPTH, STAGE_ROWS, d), F32),
            pltpu.SemaphoreType.DMA((STAGE_DEPTH,)), pltpu.SemaphoreType.DMA((STAGE_DEPTH,)),
            pltpu.VMEM((tile, d_ff), BF16), pltpu.VMEM((tile, d), F32),
        ],
        compiler_params=_params("arbitrary"),
        name="mix_ffn_ple",
    )(h, mc, ga, attn, p, gains, ple_gains, final_gain, w_attn_out, w_o, wg, wu, wd, w_gate, w_proj)


def _mix_in_kernel(h_ref, gain_ref, w_in_ref, conv_w_ref, w_conv_out_ref, cos_ref, sin_ref,
                   qt_ref, k_ref, vt_ref, mc_ref, ga_ref, u_ref, *, conv_width, attn_width):
    tile = h_ref.shape[1]
    cw, aw = conv_width, attn_width
    d = h_ref.shape[2]
    @pl.when(pl.program_id(1) == 0)
    def _():
        u_ref[0:SUBLANES, :] = jnp.zeros((SUBLANES, cw), F32)

    n, r = _prenorm(h_ref[0], gain_ref[...])

    def proj(col0, width):
        return _dot(n, _bf16(w_in_ref, cols=slice(col0, col0 + width))) * r

    b_gate, c_gate, xc = proj(0, cw), proj(cw, cw), proj(2 * cw, cw)
    u = c_gate * xc
    u_ref[SUBLANES:SUBLANES + tile, :] = u

    cos = cos_ref[0]
    sin = sin_ref[0]
    lane = lax.broadcasted_iota(jnp.int32, (1, LANES), 1)
    first_half = (lane % HEAD_DIM) < HALF_DIM
    scale = HEAD_DIM ** -0.5 * LOG2_E
    dim_row = lax.broadcasted_iota(jnp.int32, (LANES, 1), 0)
    blocks = range(tile // MOBA_BLOCK)

    def rotary(xg):
        partner = jnp.where(first_half, pltpu.roll(xg, LANES - HALF_DIM, 1), pltpu.roll(xg, HALF_DIM, 1))
        return xg * cos + partner * sin

    q_rows, k_rows, v_rows = proj(3 * cw, aw), proj(3 * cw + aw, aw), proj(3 * cw + 2 * aw, aw)
    for g in range(aw // LANES):
        q_t = (rotary(q_rows[:, g * LANES:(g + 1) * LANES]) * scale).T
        for hh in range(HEADS_PER_GROUP):
            q_th = jnp.where((dim_row // HEAD_DIM) == hh, q_t, 0.0).astype(qt_ref.dtype)
            for j in blocks:
                qt_ref[0, j, g * HEADS_PER_GROUP + hh] = q_th[:, j * MOBA_BLOCK:(j + 1) * MOBA_BLOCK]
    for g in range(aw // LANES):
        cols = slice(g * LANES, (g + 1) * LANES)
        k_ref[0, :, cols] = rotary(k_rows[:, cols]).astype(k_ref.dtype)
    ones =jnp.ones((V_ROWS - HEAD_DIM, MOBA_BLOCK), vt_ref.dtype)
    for g in range(aw // LANES):
        v_t = v_rows[:, g * LANES:(g + 1) * LANES].T.astype(vt_ref.dtype)
        for hh in range(HEADS_PER_GROUP):
            for j in blocks:
                head = g * HEADS_PER_GROUP + hh
                vt_ref[0, j, head, 0:HEAD_DIM, :] = v_t[hh * HEAD_DIM:(hh + 1) * HEAD_DIM,
                                                        j * MOBA_BLOCK:(j + 1) * MOBA_BLOCK]
                vt_ref[0, j, head, HEAD_DIM:V_ROWS, :] = ones
    base = 3 * cw + 3 * aw
    for c in range(d // PROJ_CHUNK):
        cols = slice(c * PROJ_CHUNK, (c + 1) * PROJ_CHUNK)
        ga_ref[0, :, cols] = _sigmoid(proj(base + d + c * PROJ_CHUNK, PROJ_CHUNK)).astype(ga_ref.dtype)
    conv_gate = [_sigmoid(proj(base + c * PROJ_CHUNK, PROJ_CHUNK)) for c in range(d // PROJ_CHUNK)]

    u_prev1 = u_ref[SUBLANES - 1:SUBLANES - 1 + tile, :]
    u_prev2 = u_ref[SUBLANES - 2:SUBLANES - 2 + tile, :]
    conv_w = conv_w_ref[...]
    conv = u_prev2 * conv_w[0:1, :] + u_prev1 * conv_w[1:2, :] + u * conv_w[2:3, :]
    u_ref[0:SUBLANES, :] = u[tile - SUBLANES:tile, :]
    gated = (b_gate * conv).astype(BF16)
    for c in range(d // PROJ_CHUNK):
        cols = slice(c * PROJ_CHUNK, (c + 1) * PROJ_CHUNK)
        y_conv = _dot(gated, _bf16(w_conv_out_ref, cols=cols))
        mc_ref[0, :, cols] = (conv_gate[c] * y_conv).astype(mc_ref.dtype)


def _mix_in(h, gains, w_in, conv_w, w_conv_out, cos, sin, layer):
    b, s, d = h.shape
    cw = conv_w.shape[2]
    aw = N_HEADS * HEAD_DIM
    tile = min(TOKEN_TILE, s)

    n_blocks = s // MOBA_BLOCK

    def rows(width):
        return pl.BlockSpec((1, tile, width), lambda i, j: (i, j, 0))

    def per_block(height):
        return pl.BlockSpec((1, tile // MOBA_BLOCK, N_HEADS, height, MOBA_BLOCK), lambda i, j: (i, j, 0, 0, 0))

    return pl.pallas_call(
        functools.partial(_mix_in_kernel, conv_width=cw, attn_width=aw),
        grid=(b, s // tile),
        in_specs=[rows(d), _layer(gains.shape, layer), _layer(w_in.shape, layer), _layer(conv_w.shape, layer),
                  _layer(w_conv_out.shape, layer), rows(LANES), rows(LANES)],
        out_specs=[per_block(LANES), rows(aw), per_block(V_ROWS), rows(d), rows(d)],
        out_shape=[jax.ShapeDtypeStruct((b, n_blocks, N_HEADS, LANES, MOBA_BLOCK), BF16),
                   jax.ShapeDtypeStruct((b, s, aw), BF16),
                   jax.ShapeDtypeStruct((b, n_blocks, N_HEADS, V_ROWS, MOBA_BLOCK), BF16),
                   jax.ShapeDtypeStruct((b, s, d), BF16), jax.ShapeDtypeStruct((b, s, d), BF16)],
        scratch_shapes=[pltpu.VMEM((tile + SUBLANES, cw), F32)],
        compiler_params=_params("parallel", "arbitrary"),
        name="mix_in",
    )(h, gains, w_in, conv_w, w_conv_out, cos, sin)


def _moba_kernel(qt_ref, k_ref, vt_ref, o_ref, kmean_ref, selb_ref, s0_ref, stat0_ref, s1_ref, stat1_ref, m_ref,
                 l_ref, acc_ref, *, n_blocks):
    group = pl.program_id(1)
    blk = MOBA_BLOCK
    groups = N_HEADS // HEADS_PER_GROUP

    @pl.when(group == 0)
    def _():
        for n in range(n_blocks):
            rows = slice(n * blk, (n + 1) * blk)
            kmean_ref[n:n + 1, :] = jnp.mean(k_ref[0, rows, :].astype(F32), axis=0, keepdims=True)

    blk_row = lax.broadcasted_iota(jnp.int32, (n_blocks, 1), 0)
    blk_row_f = blk_row.astype(F32)

    def select_blocks(tile, qi):
        is_past = blk_row < qi
        for g in range(groups):
            kmean = kmean_ref[:, g * LANES:(g + 1) * LANES]
            kmean_hi = kmean.astype(BF16)
            kmean_lo = (kmean - kmean_hi.astype(F32)).astype(BF16)
            for hh in range(HEADS_PER_GROUP):
                h = g * HEADS_PER_GROUP + hh
                q_th = qt_ref[0, tile, h]
                gate = _dot(kmean_hi, q_th) + _dot(kmean_lo, q_th)
                gate = jnp.where(is_past, gate, -jnp.inf)
                keep = jnp.full((n_blocks, blk), MASKED, F32)
                for _ in range(TOP_K):
                    best = jnp.max(gate, axis=0, keepdims=True)
                    first = jnp.min(jnp.where(gate == best, blk_row_f, float(n_blocks)), axis=0, keepdims=True)
                    pick = blk_row_f == first
                    keep = jnp.where(pick, 0.0, keep)
                    gate = jnp.where(pick, -jnp.inf, gate)
                selb_ref[tile, h] = jnp.where(is_past, keep, MASKED)

    slots = ((s0_ref, stat0_ref), (s1_ref, stat1_ref))

    def score_stage(tile, h, n, slot, diag):
        s_ref, stat_ref = slots[slot]
        g = h // HEADS_PER_GROUP
        start = pl.multiple_of(n * blk, blk)
        s_t = _dot(k_ref[0, pl.ds(start, blk), g * LANES:(g + 1) * LANES], qt_ref[0, tile, h])
        if diag:
            key = lax.broadcasted_iota(jnp.int32, (blk, blk), 0)
            qry = lax.broadcasted_iota(jnp.int32, (blk, blk), 1)
            s_t = jnp.where(key <= qry, s_t, MASKED)
            keep = jnp.zeros((1, blk), F32)
        else:
            keep = selb_ref[tile, h, pl.ds(n, 1), :]
        m_old = m_ref[tile, h]
        m_new = jnp.maximum(m_old, jnp.max(s_t, axis=0, keepdims=True) + keep)
        s_ref[h] = s_t
        m_ref[tile, h] = m_new
        stat_ref[h, 0:1, :] = m_new
        stat_ref[h, 1:2, :] = jnp.exp2(m_old - m_new)
        stat_ref[h, 2:3, :] = keep

    def value_stage(tile, h, n, slot):
        s_ref, stat_ref = slots[slot]
        shift = stat_ref[h, 0:1, :]
        alpha = stat_ref[h, 1:2, :]
        keep = stat_ref[h, 2:3, :]
        p_t = jnp.exp2(s_ref[h] - shift).astype(BF16)
        pv_t = _dot(vt_ref[0, n, h], p_t)
        pv_t = jnp.where(keep == 0.0, pv_t, 0.0)
        acc_ref[tile, h] = alpha * acc_ref[tile, h] + pv_t[:HEAD_DIM]
        l_ref[tile, h] = alpha * l_ref[tile, h] + pv_t[HEAD_DIM:HEAD_DIM + 1]

    def reset(tile):
        for h in range(N_HEADS):
            m_ref[tile, h] = jnp.full((1, blk), MASKED, F32)
            l_ref[tile, h] = jnp.zeros((1, blk), F32)
            acc_ref[tile, h] = jnp.zeros((HEAD_DIM, blk), F32)

    def step(tile, qi, n, waiting):
        prev = jnp.where(n == 0, qi, n - 1)
        for h in range(N_HEADS):
            score_stage(tile, h, n, 1 - waiting, False)
            value_stage(tile, h, prev, waiting)

    def write_out(tile):
        for g in range(groups):
            heads = [g * HEADS_PER_GROUP + hh for hh in range(HEADS_PER_GROUP)]
            o_t = jnp.concatenate([acc_ref[tile, h] / l_ref[tile, h] for h in heads], axis=0)
            o_ref[0, tile * blk:(tile + 1) * blk, g * LANES:(g + 1) * LANES] = o_t.T.astype(o_ref.dtype)

    base = MOBA_TILES * group
    pending = None
    first_slot = 0
    for tile in range(MOBA_TILES):
        qi = base + tile
        reset(tile)
        select_blocks(tile, qi)
        for h in range(N_HEADS):
            if pending is not None:
                value_stage(pending[0], h, pending[1], pending[2])
            score_stage(tile, h, qi, first_slot, True)
        if pending is not None:
            write_out(pending[0])

        def full_trip(t, carry, tile=tile, qi=qi, first_slot=first_slot):
            for i in range(MOBA_TILES):
                step(tile, qi, MOBA_TILES * t + i, (first_slot + i) % 2)
            return carry

        lax.fori_loop(0, group, full_trip, 0)
        for i in range(tile):
            step(tile, qi, base + i, (first_slot + i) % 2)
        last_slot = (first_slot + tile) % 2
        pending = (tile, jnp.where(qi == 0, qi, qi - 1), last_slot)
        first_slot = 1 - last_slot
    for h in range(N_HEADS):
        value_stage(pending[0], h, pending[1], pending[2])
    write_out(pending[0])


def _moba(qt, k, vt):
    b, s, aw = k.shape
    n_blocks = s // MOBA_BLOCK
    blk = MOBA_BLOCK
    tiles = MOBA_TILES
    assert n_blocks % tiles == 0 and tiles % 2 == 0
    return pl.pallas_call(
        functools.partial(_moba_kernel, n_blocks=n_blocks),
        grid=(b, n_blocks // tiles),
        in_specs=[pl.BlockSpec((1, tiles) + qt.shape[2:], lambda i, j: (i, j, 0, 0, 0)),
                  pl.BlockSpec((1, s, aw), lambda i, j: (i, 0, 0)),
                  pl.BlockSpec((1,) + vt.shape[1:], lambda i, j: (i, 0, 0, 0, 0))],
        out_specs=pl.BlockSpec((1, tiles * blk, aw), lambda i, j: (i, j, 0)),
        out_shape=jax.ShapeDtypeStruct((b, s, aw), BF16),
        scratch_shapes=[
            pltpu.VMEM((n_blocks, aw), F32),
            pltpu.VMEM((tiles, N_HEADS, n_blocks, blk), F32),
            pltpu.VMEM((N_HEADS, blk, blk), F32),
            pltpu.VMEM((N_HEADS, SUBLANES, blk), F32),
            pltpu.VMEM((N_HEADS, blk, blk), F32),
            pltpu.VMEM((N_HEADS, SUBLANES, blk), F32),
            pltpu.VMEM((tiles, N_HEADS, 1, blk), F32),
            pltpu.VMEM((tiles, N_HEADS, 1, blk), F32),
            pltpu.VMEM((tiles, N_HEADS, HEAD_DIM, blk), F32),
        ],
        compiler_params=_params("parallel", "arbitrary"),
        name="moba",
    )(qt, k, vt)


def kernel(x, p, positions, w_in, conv_w, w_conv_out, w_attn_out, w_o, ffn1_gate, ffn1_up, ffn1_down,
           ffn2_gate, ffn2_up, ffn2_down, norm_ffn1, norm_mix, norm_ffn2, norm_ple, w_ple_gate, w_ple_proj,
           norm_final):
    b, s, d = x.shape
    depth = w_in.shape[0]
    t = b * s
    assert s % MOBA_BLOCK == 0 and s % TOKEN_TILE == 0

    freq = jnp.arange(HALF_DIM, dtype=F32) / HALF_DIM
    inv_freq = jnp.tile(ROPE_THETA ** (-freq), LANES // HALF_DIM).reshape(1, LANES)

    def gains(g):
        return g.reshape(depth, 1, d)

    p_rows = p.reshape(depth, t, p.shape[-1])
    h = x.reshape(t, d)
    for i in range(depth):
        if i == 0:
            h, cos, sin = _ffn(h, gains(norm_ffn1), ffn1_gate, ffn1_up, ffn1_down, i,
                               rope=(positions.reshape(t, 1), inv_freq))
            cos, sin = cos.reshape(b, s, LANES), sin.reshape(b, s, LANES)
        else:
            h = _ffn(h, gains(norm_ffn1), ffn1_gate, ffn1_up, ffn1_down, i)
        qt, k, vt, mc, ga = _mix_in(h.reshape(b, s, d), gains(norm_mix), w_in, conv_w, w_conv_out, cos, sin, i)
        o = _moba(qt, k, vt)
        h = _mix_ffn_ple(h, mc.reshape(t, d), ga.reshape(t, d), o.reshape(t, -1), p_rows, gains(norm_ffn2),
                         gains(norm_ple), norm_final.reshape(1, d), w_attn_out, w_o, ffn2_gate, ffn2_up,
                         ffn2_down, w_ple_gate, w_ple_proj, i, i == depth - 1)
    return h.reshape(b, s, d)
```
